```python
import jax, jax.numpy as jnp
from jax import lax
import numpy as np

D_MODEL = 4096
BATCH = 1
SEQ = 8192
DEPTH = 4
DEC_BATCH = 8
DEC_SEQ = 64
PAST_LEN = 4096

CHUNK = 64
N_MIXERS = 2
N_A = (DEPTH + 1) // 2
N_B = DEPTH // 2
NH_A = 8
DK_A = D_MODEL // (2 * NH_A)
DV_A = D_MODEL // NH_A
QK_A = NH_A * DK_A
VD_A = NH_A * DV_A
A_IN = 2 * QK_A + 2 * VD_A + 2 * NH_A
GATE_SOFTCAP = 15.0
EXPAND_B = 128
NH_B = D_MODEL // EXPAND_B
DK_B = EXPAND_B
DV_B = D_MODEL // NH_B
SUB_BLOCK = 16
D_FF = ((8 * D_MODEL // 3 + 255) // 256) * 256
CONV_W = 3
ALPHA = (2 * DEPTH) ** 0.25
BETA = (8 * DEPTH) ** -0.25
LN_EPS = 1e-5
HEAD_EPS = 1e-6
NEG = -1e30

kernel_name = 'streaming_mlstm_hgrn2_convffn_step'


def layer_norm(x, g, b):
    xf = x.astype(jnp.float32)
    mu = jnp.mean(xf, axis=-1, keepdims=True)
    xc = xf - mu
    var = jnp.mean(xc * xc, axis=-1, keepdims=True)
    return (xc * lax.rsqrt(var + LN_EPS) * g.astype(jnp.float32) + b.astype(jnp.float32)).astype(x.dtype)


def _heads(a, h):
    bsz, t, _ = a.shape
    return a.reshape(bsz, t, h, -1).transpose(0, 2, 1, 3)


def _merge_heads(a):
    bsz, h, t, d = a.shape
    return a.transpose(0, 2, 1, 3).reshape(bsz, t, h * d)


def _pad_time(a, value=0.0):
    t = a.shape[2]
    tp = -(-t // CHUNK) * CHUNK
    pad = [(0, 0)] * a.ndim
    pad[2] = (0, tp - t)
    return jnp.pad(a, pad, constant_values=value)


def _to_chunks(a):
    bsz, h, t = a.shape[:3]
    a = a.reshape(bsz, h, t // CHUNK, CHUNK, *a.shape[3:])
    return jnp.moveaxis(a, 2, 0)


def _from_chunks(a):
    nc, bsz, h, l = a.shape[:4]
    return jnp.moveaxis(a, 0, 2).reshape(bsz, h, nc * l, *a.shape[4:])


def _softcap(z):
    return GATE_SOFTCAP * jnp.tanh(z / GATE_SOFTCAP)


def mlstm_chunk(carry, inp):
    C, n, m = carry
    q, k, v, ig, logf = inp
    L = q.shape[2]
    causal = jnp.tril(jnp.ones((L, L), dtype=bool))
    F = jnp.cumsum(logf, axis=-1)
    d = F[..., :, None] - F[..., None, :] + ig[..., None, :]
    d = jnp.where(causal, d, -jnp.inf)
    g = F + m[..., None]
    m_t = jnp.maximum(g, jnp.max(d, axis=-1))
    w = jnp.exp(d - m_t[..., None])
    inter = jnp.exp(g - m_t)
    s = jnp.einsum('bhtd,bhsd->bhts', q, k) * w
    num = jnp.einsum('bhts,bhsv->bhtv', s, v) + inter[..., None] * jnp.einsum('bhtd,bhdv->bhtv', q, C)
    den = jnp.sum(s, axis=-1) + inter * jnp.einsum('bhtd,bhd->bht', q, n)
    h = num / jnp.maximum(jnp.abs(den), jnp.exp(-m_t))[..., None]
    F_end = F[..., -1]
    d_end = F_end[..., None] - F + ig
    m_new = jnp.maximum(F_end + m, jnp.max(d_end, axis=-1))
    w_end = jnp.exp(d_end - m_new[..., None])
    decay = jnp.exp(F_end + m - m_new)
    C_new = decay[..., None, None] * C + jnp.einsum('bhsd,bhsv->bhdv', k * w_end[..., None], v)
    n_new = decay[..., None] * n + jnp.einsum('bhs,bhsd->bhd', w_end, k)
    return (C_new, n_new, m_new), h


def mlstm_mixer(x, C0, n0, m0, w_in, b_gate, norm_w, w_out):
    bsz, t, _ = x.shape
    f32 = jnp.float32
    proj = x @ w_in
    q = _heads(proj[..., :QK_A], NH_A).astype(f32)
    k = _heads(proj[..., QK_A:2 * QK_A], NH_A).astype(f32) * (DK_A ** -0.5)
    v = _heads(proj[..., 2 * QK_A:2 * QK_A + VD_A], NH_A).astype(f32)
    o_pre = proj[..., 2 * QK_A + VD_A:2 * QK_A + 2 * VD_A].astype(f32)
    gates = proj[..., 2 * QK_A + 2 * VD_A:].astype(f32) + b_gate.astype(f32)
    ig = _softcap(gates[..., :NH_A]).transpose(0, 2, 1)
    logf = jax.nn.log_sigmoid(_softcap(gates[..., NH_A:])).transpose(0, 2, 1)
    q, k, v, logf = _pad_time(q), _pad_time(k), _pad_time(v), _pad_time(logf)
    ig = _pad_time(ig, NEG)
    carry0 = (C0.astype(f32), n0.astype(f32), m0.astype(f32))
    (C, n, m), h = lax.scan(mlstm_chunk, carry0, tuple(_to_chunks(a) for a in (q, k, v, ig, logf)))
    h = _from_chunks(h)[:, :, :t]
    mu = jnp.mean(h, axis=-1, keepdims=True)
    hc = h - mu
    h = hc * lax.rsqrt(jnp.mean(hc * hc, axis=-1, keepdims=True) + HEAD_EPS)
    h = _merge_heads(h) * norm_w.astype(f32)
    y = (jax.nn.sigmoid(o_pre) * h).astype(x.dtype) @ w_out
    return y, C.astype(x.dtype), n.astype(x.dtype), m.astype(x.dtype)


def hgrn_chunk(S, inp):
    q, k, i, logf = inp
    bsz, h, L, dk = q.shape
    ns = L // SUB_BLOCK
    b = jnp.cumsum(logf, axis=2)
    o = jnp.einsum('bhtd,bhdv->bhtv', q * jnp.exp(b), S)
    qr = q.reshape(bsz, h, ns, SUB_BLOCK, dk)
    kr = k.reshape(bsz, h, ns, SUB_BLOCK, dk)
    br = b.reshape(bsz, h, ns, SUB_BLOCK, dk)
    ir = i.reshape(bsz, h, ns, SUB_BLOCK, -1)
    r = jnp.concatenate([jnp.zeros_like(b[:, :, :1]), b[:, :, SUB_BLOCK - 1:L - 1:SUB_BLOCK]], axis=2)
    qa = qr * jnp.exp(br - r[:, :, :, None, :])
    ka = k[:, :, None] * jnp.exp(jnp.minimum(r[:, :, :, None, :] - b[:, :, None], 0.0))
    pos = jnp.arange(L)
    start = jnp.arange(ns) * SUB_BLOCK
    off_mask = (pos[None, :] < start[:, None]).astype(q.dtype)
    s_off = jnp.einsum('bhatd,bhasd->bhats', qa, ka) * off_mask[:, None, :]
    o_off = jnp.einsum('bhats,bhsv->bhatv', s_off, i)
    tri = jnp.tril(jnp.ones((SUB_BLOCK, SUB_BLOCK), dtype=bool))
    dd = br[:, :, :, :, None, :] - br[:, :, :, None, :, :]
    dd = jnp.exp(jnp.where(tri[:, :, None], dd, -jnp.inf))
    s_diag = jnp.einsum('bhatd,bhatsd,bhasd->bhats', qr, dd, kr)
    o_diag = jnp.einsum('bhats,bhasv->bhatv', s_diag, ir)
    o = o + (o_off + o_diag).reshape(bsz, h, L, -1)
    b_end = b[:, :, -1]
    S_new = jnp.exp(b_end)[..., None] * S + jnp.einsum('bhsd,bhsv->bhdv', k * jnp.exp(b_end[:, :, None] - b), i)
    return S_new, o


def hgrn2_mixer(x, S0, lb, w_in, norm_w, w_out):
    bsz, t, _ = x.shape
    f32 = jnp.float32
    proj = x @ w_in
    q = proj[..., :D_MODEL].astype(f32)
    fz = proj[..., D_MODEL:2 * D_MODEL].astype(f32)
    inp = proj[..., 2 * D_MODEL:3 * D_MODEL].astype(f32)
    gate = proj[..., 3 * D_MODEL:].astype(f32)
    logf = jnp.logaddexp(jnp.log(lb), jnp.log1p(-lb) + jax.nn.log_sigmoid(fz))
    k = -jnp.expm1(logf)
    q, k, logf = _heads(q, NH_B), _heads(k, NH_B), _heads(logf, NH_B)
    inp = _heads(inp, NH_B)
    q, k, inp, logf = _pad_time(q), _pad_time(k), _pad_time(inp), _pad_time(logf)
    S, o = lax.scan(hgrn_chunk, S0.astype(f32), tuple(_to_chunks(a) for a in (q, k, inp, logf)))
    o = _from_chunks(o)[:, :, :t]
    o = o * lax.rsqrt(jnp.mean(o * o, axis=-1, keepdims=True) + HEAD_EPS)
    o = _merge_heads(o) * norm_w.astype(f32) * jax.nn.silu(gate)
    y = o.astype(x.dtype) @ w_out
    return y, S.astype(x.dtype)


def conv_ffn(x, conv_state, w_gate, w_up, w_down, conv_w, conv_b):
    t = x.shape[1]
    a = x @ w_gate
    u = x @ w_up
    a_full = jnp.concatenate([conv_state.astype(a.dtype), a], axis=1)
    c = conv_b.astype(a.dtype)
    for j in range(CONV_W):
        c = c + a_full[:, j:j + t] * conv_w[j]
    h = jax.nn.silu(c) * u
    return h @ w_down, a_full[:, t:]


def hgrn_lower_bounds(b_lower):
    p = jax.nn.softmax(b_lower.astype(jnp.float32), axis=0)
    c = jnp.cumsum(p, axis=0)
    return c - c[0]


def setup_inputs(seed: int = 0) -> dict:
    key = jax.random.key(seed)
    ks = jax.random.split(key, 24)
    nrm = jax.random.normal
    f32 = jnp.float32
    s_d = D_MODEL ** -0.5
    b_gate = jnp.concatenate([
        0.1 * nrm(ks[10], (N_A, NH_A), f32),
        jnp.linspace(3.0, 6.0, NH_A, dtype=f32)[None, :] + 0.1 * nrm(ks[11], (N_A, NH_A), f32)], axis=1)
    return {
        'x_prompt': nrm(ks[0], (BATCH, SEQ, D_MODEL), f32),
        'x_sample': nrm(ks[1], (DEC_BATCH, DEC_SEQ, D_MODEL), f32),
        'state_mlstm_C': 0.1 * nrm(ks[2], (N_A, DEC_BATCH, NH_A, DK_A, DV_A), f32),
        'state_mlstm_n': 0.1 * nrm(ks[3], (N_A, DEC_BATCH, NH_A, DK_A), f32),
        'state_mlstm_m': 1.0 + 0.5 * nrm(ks[4], (N_A, DEC_BATCH, NH_A), f32),
        'state_hgrn_S': 0.5 * nrm(ks[5], (N_B, DEC_BATCH, NH_B, DK_B, DV_B), f32),
        'state_ffn_conv': nrm(ks[6], (DEPTH, DEC_BATCH, CONV_W - 1, D_FF), f32),
        'ln_g': 1.0 + 0.02 * nrm(ks[7], (DEPTH, 2, D_MODEL), f32),
        'ln_b': 0.02 * nrm(ks[8], (DEPTH, 2, D_MODEL), f32),
        'w_a_in': s_d * nrm(ks[9], (N_A, D_MODEL, A_IN), f32),
        'b_a_gate': b_gate,
        'a_norm_w': 1.0 + 0.02 * nrm(ks[12], (N_A, D_MODEL), f32),
        'w_a_out': BETA * s_d * nrm(ks[13], (N_A, D_MODEL, D_MODEL), f32),
        'w_b_in': s_d * nrm(ks[14], (N_B, D_MODEL, 4 * D_MODEL), f32),
        'b_lower': 0.1 * nrm(ks[15], (N_B, D_MODEL), f32),
        'b_norm_w': 1.0 + 0.02 * nrm(ks[16], (N_B, D_MODEL), f32),
        'w_b_out': BETA * s_d * nrm(ks[17], (N_B, D_MODEL, D_MODEL), f32),
        'w_ffn_gate': s_d * nrm(ks[18], (DEPTH, D_MODEL, D_FF), f32),
        'w_ffn_up': s_d * nrm(ks[19], (DEPTH, D_MODEL, D_FF), f32),
        'w_ffn_down': BETA * (D_FF ** -0.5) * nrm(ks[20], (DEPTH, D_FF, D_MODEL), f32),
        'ffn_conv_w': (CONV_W ** -0.5) * nrm(ks[21], (DEPTH, CONV_W, D_FF), f32),
        'ffn_conv_b': 0.02 * nrm(ks[22], (DEPTH, D_FF), f32),
    }


def reference(x_prompt, x_sample, state_mlstm_C, state_mlstm_n, state_mlstm_m, state_hgrn_S, state_ffn_conv,
              ln_g, ln_b, w_a_in, b_a_gate, a_norm_w, w_a_out, w_b_in, b_lower, b_norm_w, w_b_out,
              w_ffn_gate, w_ffn_up, w_ffn_down, ffn_conv_w, ffn_conv_b):
    f32 = jnp.float32
    lb_all = hgrn_lower_bounds(b_lower)
    xp, xs = x_prompt, x_sample
    bp = xp.shape[0]
    pC, pn, pm, pS, pconv = [], [], [], [], []
    sC, sn, sm, sS, sconv = [], [], [], [], []
    for l in range(DEPTH):
        j = l // N_MIXERS
        if l % N_MIXERS == 0:
            yp, c_, n_, m_ = mlstm_mixer(xp, jnp.zeros((bp, NH_A, DK_A, DV_A), f32), jnp.zeros((bp, NH_A, DK_A), f32),
                                         jnp.zeros((bp, NH_A), f32), w_a_in[j], b_a_gate[j], a_norm_w[j], w_a_out[j])
            pC.append(c_); pn.append(n_); pm.append(m_)
            ys, c_, n_, m_ = mlstm_mixer(xs, state_mlstm_C[j], state_mlstm_n[j], state_mlstm_m[j],
                                         w_a_in[j], b_a_gate[j], a_norm_w[j], w_a_out[j])
            sC.append(c_); sn.append(n_); sm.append(m_)
        else:
            yp, s_ = hgrn2_mixer(xp, jnp.zeros((bp, NH_B, DK_B, DV_B), f32), lb_all[j], w_b_in[j], b_norm_w[j], w_b_out[j])
            pS.append(s_)
            ys, s_ = hgrn2_mixer(xs, state_hgrn_S[j], lb_all[j], w_b_in[j], b_norm_w[j], w_b_out[j])
            sS.append(s_)
        xp = layer_norm(ALPHA * xp + yp, ln_g[l, 0], ln_b[l, 0])
        xs = layer_norm(ALPHA * xs + ys, ln_g[l, 0], ln_b[l, 0])
        fp, cv = conv_ffn(xp, jnp.zeros((bp, CONV_W - 1, D_FF), xp.dtype), w_ffn_gate[l], w_ffn_up[l],
                          w_ffn_down[l], ffn_conv_w[l], ffn_conv_b[l])
        pconv.append(cv)
        fs, cv = conv_ffn(xs, state_ffn_conv[l], w_ffn_gate[l], w_ffn_up[l], w_ffn_down[l], ffn_conv_w[l], ffn_conv_b[l])
        sconv.append(cv)
        xp = layer_norm(ALPHA * xp + fp, ln_g[l, 1], ln_b[l, 1])
        xs = layer_norm(ALPHA * xs + fs, ln_g[l, 1], ln_b[l, 1])
    p_C, p_n, p_m, p_S, p_conv = jnp.stack(pC), jnp.stack(pn), jnp.stack(pm), jnp.stack(pS), jnp.stack(pconv)
    s_C, s_n, s_m, s_S, s_conv = jnp.stack(sC), jnp.stack(sn), jnp.stack(sm), jnp.stack(sS), jnp.stack(sconv)
    return (xp, xs, p_C, p_n, p_m, p_S, p_conv, s_C, s_n, s_m, s_S, s_conv)
```

```python
import functools

import jax
import jax.numpy as jnp
from jax import lax
from jax.experimental import pallas as pl
from jax.experimental.pallas import tpu as pltpu

F32 = jnp.float32
BF16 = jnp.bfloat16

D_MODEL = 4096
DEPTH = 4
CHUNK = 64
NH_A = 8
DK_A = 256
DV_A = 512
QK_A = NH_A * DK_A
VD_A = NH_A * DV_A
A_MAIN = 2 * QK_A + 2 * VD_A
GATE_SOFTCAP = 15.0
NH_B = 32
DK_B = 128
DV_B = 128
D_FF = 11008
D_FF_PAD = 11264
CONV_W = 3
ALPHA = (2 * DEPTH) ** 0.25
LN_EPS = 1e-5
HEAD_EPS = 1e-6

LANES = 128
SUBLANES = 8
VMEM_LIMIT_CAP = 58 * 1024 * 1024


def _vmem_limit(nbytes):
    return int(min(VMEM_LIMIT_CAP, nbytes * 1.2 + (6 << 20)))


def _nbytes(shape, dtype):
    n = 1
    for s in shape:
        n *= s
    return n * jnp.dtype(dtype).itemsize


def _mm_kernel(*refs, nk, has_res, alpha):
    if has_res:
        x_ref, w_ref, r_ref, o_ref = refs[:4]
        scratch = refs[4:]
    else:
        x_ref, w_ref, o_ref = refs[:3]
        r_ref = None
        scratch = refs[3:]

    def finish(acc):
        if has_res:
            acc = alpha * r_ref[...] + acc
        o_ref[...] = acc.astype(o_ref.dtype)

    part = jnp.dot(x_ref[...], w_ref[...], preferred_element_type=F32)
    if nk == 1:
        finish(part)
    else:
        acc_ref = scratch[0]
        k = pl.program_id(2)

        @pl.when(k == 0)
        def _():
            acc_ref[...] = part

        @pl.when(k > 0)
        def _():
            acc_ref[...] += part

        @pl.when(k == nk - 1)
        def _():
            finish(acc_ref[...])


def matmul(x, w, *, bm, bn, bk=None, res=None, alpha=1.0, out_dtype=F32, n_outer=True, name="mm"):
    m, kdim = x.shape
    _, n = w.shape
    bk = kdim if bk is None else bk
    assert m % bm == 0 and n % bn == 0 and kdim % bk == 0
    nm, nn, nk = m // bm, n // bn, kdim // bk
    if n_outer:
        grid = (nn, nm, nk)
        xi = lambda j, i, k: (i, k)
        wi = lambda j, i, k: (k, j)
        oi = lambda j, i, k: (i, j)
    else:
        grid = (nm, nn, nk)
        xi = lambda i, j, k: (i, k)
        wi = lambda i, j, k: (k, j)
        oi = lambda i, j, k: (i, j)
    in_specs = [pl.BlockSpec((bm, bk), xi), pl.BlockSpec((bk, bn), wi)]
    args = [x, w]
    est = 2 * (_nbytes((bm, bk), x.dtype) + _nbytes((bk, bn), w.dtype) + _nbytes((bm, bn), out_dtype))
    est += _nbytes((bm, bn), F32)
    if res is not None:
        in_specs.append(pl.BlockSpec((bm, bn), oi))
        args.append(res)
        est += 2 * _nbytes((bm, bn), res.dtype)
    scratch = []
    if nk > 1:
        scratch.append(pltpu.VMEM((bm, bn), F32))
        est += _nbytes((bm, bn), F32)
    return pl.pallas_call(
        functools.partial(_mm_kernel, nk=nk, has_res=res is not None, alpha=alpha),
        grid=grid,
        in_specs=in_specs,
        out_specs=pl.BlockSpec((bm, bn), oi),
        out_shape=jax.ShapeDtypeStruct((m, n), out_dtype),
        scratch_shapes=scratch,
        compiler_params=pltpu.CompilerParams(
            dimension_semantics=("arbitrary",) * 3, vmem_limit_bytes=_vmem_limit(est)),
        name=name,
    )(*args)


def _ln_kernel(y_ref, g_ref, b_ref, o_ref, ob_ref):
    y = y_ref[...]
    mu = jnp.mean(y, axis=-1, keepdims=True)
    yc = y - mu
    var = jnp.mean(yc * yc, axis=-1, keepdims=True)
    out = yc * lax.rsqrt(var + LN_EPS) * g_ref[...] + b_ref[...]
    o_ref[...] = out
    ob_ref[...] = out.astype(BF16)


def layer_norm(y, g, b, *, bm=256):
    m, d = y.shape
    row = pl.BlockSpec((bm, d), lambda i: (i, 0))
    vec = pl.BlockSpec((1, d), lambda i: (0, 0))
    est = 2 * (2 * _nbytes((bm, d), F32) + _nbytes((bm, d), BF16)) + 2 * _nbytes((bm, d), F32)
    return pl.pallas_call(
        _ln_kernel,
        grid=(m // bm,),
        in_specs=[row, vec, vec],
        out_specs=[row, row],
        out_shape=[jax.ShapeDtypeStruct((m, d), F32), jax.ShapeDtypeStruct((m, d), BF16)],
        compiler_params=pltpu.CompilerParams(
            dimension_semantics=("arbitrary",), vmem_limit_bytes=_vmem_limit(est)),
        name="layer_norm",
    )(y, g.reshape(1, d), b.reshape(1, d))


def _log_sigmoid(z):
    return jnp.minimum(z, 0.0) - jnp.log1p(jnp.exp(-jnp.abs(z)))


def _gates_kernel(x_ref, w_ref, b_ref, o_ref):
    g = jnp.dot(x_ref[...], w_ref[...], preferred_element_type=F32) + b_ref[...]
    sc = GATE_SOFTCAP * jnp.tanh(g / GATE_SOFTCAP)
    lane = lax.broadcasted_iota(jnp.int32, sc.shape, 1)
    o_ref[...] = jnp.where(lane < NH_A, sc, _log_sigmoid(sc))


def mlstm_gates(xb, w_gate, b_gate, *, bm=1088):
    m, d = xb.shape
    est = 2 * (_nbytes((bm, d), BF16) + _nbytes((d, LANES), BF16) + _nbytes((bm, LANES), F32))
    return pl.pallas_call(
        _gates_kernel,
        grid=(m // bm,),
        in_specs=[pl.BlockSpec((bm, d), lambda i: (i, 0)),
                  pl.BlockSpec((d, LANES), lambda i: (0, 0)),
                  pl.BlockSpec((1, LANES), lambda i: (0, 0))],
        out_specs=pl.BlockSpec((bm, LANES), lambda i: (i, 0)),
        out_shape=jax.ShapeDtypeStruct((m, LANES), F32),
        compiler_params=pltpu.CompilerParams(
            dimension_semantics=("arbitrary",), vmem_limit_bytes=_vmem_limit(est)),
        name="mlstm_gates",
    )(xb, w_gate, b_gate)


def _mlstm_kernel(*refs, L, has_init):
    if has_init:
        (q_ref, k_ref, v_ref, o_ref, gc_ref, gr_ref, nw_ref, c0_ref, n0_ref, m0_ref,
         y_ref, ct_ref, nt_ref, mt_ref, c_s, n_s, m_s) = refs
    else:
        (q_ref, k_ref, v_ref, o_ref, gc_ref, gr_ref, nw_ref,
         y_ref, ct_ref, nt_ref, mt_ref, c_s, n_s, m_s) = refs
    c = pl.program_id(1)
    nc = pl.num_programs(1)

    @pl.when(c == 0)
    def _():
        if has_init:
            c_s[...] = c0_ref[0]
            n_s[...] = n0_ref[0]
            m_s[...] = m0_ref[0]
        else:
            c_s[...] = jnp.zeros_like(c_s)
            n_s[...] = jnp.zeros_like(n_s)
            m_s[...] = jnp.zeros_like(m_s)

    t_idx = lax.broadcasted_iota(jnp.int32, (L, L), 0)
    s_idx = lax.broadcasted_iota(jnp.int32, (L, L), 1)
    causal = s_idx <= t_idx
    lane = lax.broadcasted_iota(jnp.int32, (L, LANES), 1)
    gcol = gc_ref[...]

    def head(h, carry):
        qs = pl.ds(pl.multiple_of(h * DK_A, DK_A), DK_A)
        vs = pl.ds(pl.multiple_of(h * DV_A, DV_A), DV_A)
        q = q_ref[:, qs]
        k = k_ref[:, qs] * (DK_A ** -0.5)
        v = v_ref[:, vs]
        qb = q.astype(BF16)
        kb = k.astype(BF16)
        vb = v.astype(BF16)
        ig_r = gr_ref[h, 0]
        lf_r = gr_ref[NH_A + h, 0]
        ig_c = jnp.sum(jnp.where(lane == h, gcol, 0.0), axis=1, keepdims=True)
        lf_c = jnp.sum(jnp.where(lane == NH_A + h, gcol, 0.0), axis=1, keepdims=True)
        f_c = jnp.sum(jnp.where(causal, lf_r, 0.0), axis=1, keepdims=True)
        f_r = jnp.sum(jnp.where(t_idx <= s_idx, lf_c, 0.0), axis=0, keepdims=True)
        f_end = jnp.sum(lf_r, axis=1, keepdims=True)
        m_prev = m_s[h]
        c_prev = c_s[h]
        n_prev = n_s[h]

        d = jnp.where(causal, f_c - f_r + ig_r, -jnp.inf)
        g = f_c + m_prev
        m_t = jnp.maximum(g, jnp.max(d, axis=1, keepdims=True))
        w = jnp.exp(d - m_t)
        inter = jnp.exp(g - m_t)
        s = lax.dot_general(qb, kb, (((1,), (1,)), ((), ())), preferred_element_type=F32) * w
        num = jnp.dot(s.astype(BF16), vb, preferred_element_type=F32)
        num = num + inter * jnp.dot(qb, c_prev.astype(BF16), preferred_element_type=F32)
        den = jnp.sum(s, axis=1, keepdims=True) + inter * jnp.sum(q * n_prev, axis=1, keepdims=True)
        hh = num / jnp.maximum(jnp.abs(den), jnp.exp(-m_t))

        d_end_r = f_end - f_r + ig_r
        d_end_c = f_end - f_c + ig_c
        m_new = jnp.maximum(f_end + m_prev, jnp.max(d_end_r, axis=1, keepdims=True))
        w_end_c = jnp.exp(d_end_c - m_new)
        decay = jnp.exp(f_end + m_prev - m_new)
        kw = k * w_end_c
        c_s[h] = decay * c_prev + lax.dot_general(
            kw.astype(BF16), vb, (((0,), (0,)), ((), ())), preferred_element_type=F32)
        n_s[h] = decay * n_prev + jnp.sum(kw, axis=0, keepdims=True)
        m_s[h] = m_new

        mu = jnp.mean(hh, axis=1, keepdims=True)
        hc = hh - mu
        hn = hc * lax.rsqrt(jnp.mean(hc * hc, axis=1, keepdims=True) + HEAD_EPS)
        hn = hn * nw_ref[:, vs]
        y_ref[:, vs] = (jax.nn.sigmoid(o_ref[:, vs]) * hn).astype(BF16)
        return carry

    lax.fori_loop(0, NH_A, head, 0)

    @pl.when(c == nc - 1)
    def _():
        ct_ref[0] = c_s[...]
        nt_ref[0] = n_s[...]
        mt_ref[0] = m_s[...]


def mlstm_scan(proj, gates, norm_w, *, row0, n_seq, n_chunk, L, init=None):
    rb0 = row0 // L
    rows = n_seq * n_chunk * L
    g_rows = lax.slice_in_dim(gates, row0, row0 + rows, axis=0)
    g_t = g_rows[:, :2 * NH_A].T.reshape(2 * NH_A, n_seq * n_chunk, 1, L)
    rmap = lambda s, c: (rb0 + s * n_chunk + c)
    in_specs = [
        pl.BlockSpec((L, QK_A), lambda s, c: (rmap(s, c), 0)),
        pl.BlockSpec((L, QK_A), lambda s, c: (rmap(s, c), 1)),
        pl.BlockSpec((L, VD_A), lambda s, c: (rmap(s, c), 1)),
        pl.BlockSpec((L, VD_A), lambda s, c: (rmap(s, c), 2)),
        pl.BlockSpec((L, LANES), lambda s, c: (rmap(s, c), 0)),
        pl.BlockSpec((2 * NH_A, 1, 1, L), lambda s, c: (0, s * n_chunk + c, 0, 0)),
        pl.BlockSpec((1, VD_A), lambda s, c: (0, 0)),
    ]
    args = [proj, proj, proj, proj, gates, g_t, norm_w.reshape(1, VD_A)]
    state_specs = [
        pl.BlockSpec((1, NH_A, DK_A, DV_A), lambda s, c: (s, 0, 0, 0)),
        pl.BlockSpec((1, NH_A, 1, DK_A), lambda s, c: (s, 0, 0, 0)),
        pl.BlockSpec((1, NH_A, 1, 1), lambda s, c: (s, 0, 0, 0)),
    ]
    if init is not None:
        c0, n0, m0 = init
        in_specs += state_specs
        args += [c0, n0.reshape(n_seq, NH_A, 1, DK_A), m0.reshape(n_seq, NH_A, 1, 1)]
    est = 2 * (2 * _nbytes((L, QK_A), F32) + 2 * _nbytes((L, VD_A), F32) + _nbytes((L, VD_A), BF16))
    est += (3 if init is None else 5) * _nbytes((NH_A, DK_A, DV_A), F32)
    est += 8 * _nbytes((L, max(L, DV_A)), F32)
    y, ct, nt, mt = pl.pallas_call(
        functools.partial(_mlstm_kernel, L=L, has_init=init is not None),
        grid=(n_seq, n_chunk),
        in_specs=in_specs,
        out_specs=[pl.BlockSpec((L, VD_A), lambda s, c: (s * n_chunk + c, 0))] + state_specs,
        out_shape=[
            jax.ShapeDtypeStruct((rows, VD_A), BF16),
            jax.ShapeDtypeStruct((n_seq, NH_A, DK_A, DV_A), F32),
            jax.ShapeDtypeStruct((n_seq, NH_A, 1, DK_A), F32),
            jax.ShapeDtypeStruct((n_seq, NH_A, 1, 1), F32),
        ],
        scratch_shapes=[
            pltpu.VMEM((NH_A, DK_A, DV_A), F32),
            pltpu.VMEM((NH_A, 1, DK_A), F32),
            pltpu.VMEM((NH_A, 1, 1), F32),
        ],
        compiler_params=pltpu.CompilerParams(
            dimension_semantics=("arbitrary", "arbitrary"), vmem_limit_bytes=_vmem_limit(est)),
        name="mlstm_scan_init" if init is not None else "mlstm_scan",
    )(*args)
    return y, ct, nt.reshape(n_seq, NH_A, DK_A), mt.reshape(n_seq, NH_A)


def _hgrn_kernel(q_ref, f_ref, i_ref, g_ref, bl_ref, nw_ref, s0_ref, y_ref, st_ref, s_s,
                 *, layer_j, n_prompt_chunks):
    L = CHUNK
    g_id = pl.program_id(0)

    @pl.when(g_id == 0)
    def _():
        s_s[...] = jnp.zeros_like(s_s)

    @pl.when(g_id >= n_prompt_chunks)
    def _():
        def load(h, carry):
            s_s[h] = s0_ref[0, h].T
            return carry
        lax.fori_loop(0, NH_B, load, 0)

    row = lax.broadcasted_iota(jnp.int32, (L, DK_B), 0)
    t_idx = lax.broadcasted_iota(jnp.int32, (L, L), 0)
    s_idx = lax.broadcasted_iota(jnp.int32, (L, L), 1)
    levels = [1, 2, 4, 8, 16, 32]
    upper = {mm: (row & mm) != 0 for mm in levels}
    pair = {mm: ((t_idx ^ s_idx) < 2 * mm) & ((t_idx & mm) != 0) & ((s_idx & mm) == 0) for mm in levels}
    diag = t_idx == s_idx
    nt_dims = (((1,), (1,)), ((), ()))
    tn_dims = (((0,), (0,)), ((), ()))

    def head(h, carry):
        sl = pl.ds(pl.multiple_of(h * DK_B, DK_B), DK_B)
        q = q_ref[:, sl]
        fz = f_ref[:, sl]
        inp = i_ref[:, sl]
        gate = g_ref[:, sl]
        bl = bl_ref[:, sl]
        e = jnp.exp(bl - jnp.max(bl, axis=0, keepdims=True))
        p = e / jnp.sum(e, axis=0, keepdims=True)
        lb = jnp.zeros((1, DK_B), F32)
        for r in range(1, layer_j + 1):
            lb = lb + p[r:r + 1]
        a1 = jnp.log(lb)
        a2 = jnp.log1p(-lb) + _log_sigmoid(fz)
        logf = jnp.maximum(a1, a2) + jnp.log1p(jnp.exp(-jnp.abs(a1 - a2)))
        k = (1.0 - lb) * jax.nn.sigmoid(-fz)

        ib = inp.astype(BF16)
        s_acc = jnp.where(
            diag, lax.dot_general(q.astype(BF16), k.astype(BF16), nt_dims, preferred_element_type=F32), 0.0)
        p_sum = logf
        g_sum = logf
        for mm in levels:
            up = upper[mm]
            sib = jnp.where(up, pltpu.roll(g_sum, mm, 0), pltpu.roll(g_sum, L - mm, 0))
            dec = jnp.exp(jnp.where(up, p_sum, g_sum - p_sum))
            s_m = lax.dot_general((q * dec).astype(BF16), (k * dec).astype(BF16), nt_dims,
                                  preferred_element_type=F32)
            s_acc = s_acc + jnp.where(pair[mm], s_m, 0.0)
            p_sum = p_sum + jnp.where(up, sib, 0.0)
            g_sum = g_sum + sib
        s_prev = s_s[h]
        o = lax.dot_general((q * jnp.exp(p_sum)).astype(BF16), s_prev.astype(BF16), nt_dims,
                            preferred_element_type=F32)
        o = o + jnp.dot(s_acc.astype(BF16), ib, preferred_element_type=F32)
        k_end = (k * jnp.exp(g_sum - p_sum)).astype(BF16)
        s_s[h] = jnp.exp(g_sum[0:1]) * s_prev + lax.dot_general(ib, k_end, tn_dims, preferred_element_type=F32)

        o = o * lax.rsqrt(jnp.mean(o * o, axis=1, keepdims=True) + HEAD_EPS)
        o = o * nw_ref[:, sl] * (gate * jax.nn.sigmoid(gate))
        y_ref[:, sl] = o.astype(BF16)
        return carry

    lax.fori_loop(0, NH_B, head, 0)

    @pl.when(g_id >= n_prompt_chunks - 1)
    def _():
        def store(h, carry):
            st_ref[0, h] = s_s[h].T
            return carry
        lax.fori_loop(0, NH_B, store, 0)


def hgrn_scan(proj, b_lower, norm_w, s0, *, layer_j, n_prompt_chunks):
    m = proj.shape[0]
    n_chunks = m // CHUNK
    n_seq = n_chunks - n_prompt_chunks + 1
    n_b = b_lower.shape[0]
    col = lambda cb: pl.BlockSpec((CHUNK, D_MODEL), lambda g: (g, cb))
    est = 2 * (4 * _nbytes((CHUNK, D_MODEL), F32) + _nbytes((CHUNK, D_MODEL), BF16))
    est += 5 * _nbytes((NH_B, DK_B, DV_B), F32)
    y, st = pl.pallas_call(
        functools.partial(_hgrn_kernel, layer_j=layer_j, n_prompt_chunks=n_prompt_chunks),
        grid=(n_chunks,),
        in_specs=[col(0), col(1), col(2), col(3),
                  pl.BlockSpec((n_b, D_MODEL), lambda g: (0, 0)),
                  pl.BlockSpec((1, D_MODEL), lambda g: (0, 0)),
                  pl.BlockSpec((1, NH_B, DK_B, DV_B),
                               lambda g: (jnp.maximum(g - n_prompt_chunks, 0), 0, 0, 0))],
        out_specs=[pl.BlockSpec((CHUNK, D_MODEL), lambda g: (g, 0)),
                   pl.BlockSpec((1, NH_B, DK_B, DV_B),
                                lambda g: (jnp.maximum(g - (n_prompt_chunks - 1), 0), 0, 0, 0))],
        out_shape=[jax.ShapeDtypeStruct((m, D_MODEL), BF16),
                   jax.ShapeDtypeStruct((n_seq, NH_B, DK_B, DV_B), F32)],
        scratch_shapes=[pltpu.VMEM((NH_B, DV_B, DK_B), F32)],
        compiler_params=pltpu.CompilerParams(
            dimension_semantics=("arbitrary",), vmem_limit_bytes=_vmem_limit(est)),
        name="hgrn_scan",
    )(proj, proj, proj, proj, b_lower, norm_w.reshape(1, D_MODEL), s0)
    return y, st


def _ffn_kernel(x_ref, wg_ref, wu_ref, cw_ref, cb_ref, st_ref, h_ref, so_ref, carry_ref,
                *, bm, n_prompt_tiles, seq_len):
    i = pl.program_id(1)
    x = x_ref[...]
    a = jnp.dot(x, wg_ref[...], preferred_element_type=F32)
    u = jnp.dot(x, wu_ref[...], preferred_element_type=F32)
    w0 = cw_ref[0:1, :]
    w1 = cw_ref[1:2, :]
    w2 = cw_ref[2:3, :]
    cb = cb_ref[...]

    def act(a_cur, a_p1, a_p2, u_cur):
        c = cb + a_p2 * w0
        c = c + a_p1 * w1
        c = c + a_cur * w2
        return (c * jax.nn.sigmoid(c) * u_cur).astype(BF16)

    h_ref[...] = act(a, pltpu.roll(a, 1, 0), pltpu.roll(a, 2, 0), u)

    rows8 = lax.broadcasted_iota(jnp.int32, (SUBLANES, a.shape[1]), 0)

    def patch(r0, prev8):
        blk = a[r0:r0 + SUBLANES]
        p1 = jnp.where(rows8 < 1, pltpu.roll(prev8, 1, 0), pltpu.roll(blk, 1, 0))
        p2 = jnp.where(rows8 < 2, pltpu.roll(prev8, 2, 0), pltpu.roll(blk, 2, 0))
        h_ref[r0:r0 + SUBLANES, :] = act(blk, p1, p2, u[r0:r0 + SUBLANES])

    is_sample = i >= n_prompt_tiles

    @pl.when(i == 0)
    def _():
        carry_ref[...] = jnp.zeros_like(carry_ref)

    patch(0, jnp.where(is_sample, st_ref[0], carry_ref[...]))
    n_groups = bm // seq_len

    @pl.when(is_sample)
    def _():
        for s in range(1, n_groups):
            patch(s * seq_len, st_ref[s])
        for s in range(n_groups):
            so_ref[1 + s] = a[(s + 1) * seq_len - SUBLANES:(s + 1) * seq_len]

    carry_ref[...] = a[bm - SUBLANES:bm]

    @pl.when(i == n_prompt_tiles - 1)
    def _():
        so_ref[0] = a[bm - SUBLANES:bm]


def ffn_gate_up(xb, wg, wu, conv_w, conv_b, state8, *, bm, bn, n_prompt_tiles, seq_len):
    m, d = xb.shape
    n = wg.shape[1]
    n_seq_out = state8.shape[0] + 1
    est = 2 * (_nbytes((bm, d), BF16) + 2 * _nbytes((d, bn), BF16) + _nbytes((bm, bn), BF16))
    est += 4 * _nbytes((bm, bn), F32) + 4 * _nbytes((n_seq_out, SUBLANES, bn), F32)
    return pl.pallas_call(
        functools.partial(_ffn_kernel, bm=bm, n_prompt_tiles=n_prompt_tiles, seq_len=seq_len),
        grid=(n // bn, m // bm),
        in_specs=[pl.BlockSpec((bm, d), lambda j, i: (i, 0)),
                  pl.BlockSpec((d, bn), lambda j, i: (0, j)),
                  pl.BlockSpec((d, bn), lambda j, i: (0, j)),
                  pl.BlockSpec((CONV_W, bn), lambda j, i: (0, j)),
                  pl.BlockSpec((1, bn), lambda j, i: (0, j)),
                  pl.BlockSpec((state8.shape[0], SUBLANES, bn), lambda j, i: (0, 0, j))],
        out_specs=[pl.BlockSpec((bm, bn), lambda j, i: (i, j)),
                   pl.BlockSpec((n_seq_out, SUBLANES, bn), lambda j, i: (0, 0, j))],
        out_shape=[jax.ShapeDtypeStruct((m, n), BF16),
                   jax.ShapeDtypeStruct((n_seq_out, SUBLANES, n), F32)],
        scratch_shapes=[pltpu.VMEM((SUBLANES, bn), F32)],
        compiler_params=pltpu.CompilerParams(
            dimension_semantics=("arbitrary", "arbitrary"), vmem_limit_bytes=_vmem_limit(est)),
        name="ffn_gate_up",
    )(xb, wg, wu, conv_w, conv_b, state8)


def kernel(x_prompt, x_sample, state_mlstm_C, state_mlstm_n, state_mlstm_m, state_hgrn_S, state_ffn_conv,
           ln_g, ln_b, w_a_in, b_a_gate, a_norm_w, w_a_out, w_b_in, b_lower, b_norm_w, w_b_out,
           w_ffn_gate, w_ffn_up, w_ffn_down, ffn_conv_w, ffn_conv_b):
    bp, tp, d = x_prompt.shape
    bs, ts, _ = x_sample.shape
    mp = bp * tp
    ms = bs * ts
    m = mp + ms
    assert bp == 1 and ts == CHUNK and d == D_MODEL
    ffn_bm = ms
    assert mp % ffn_bm == 0
    l_prompt = 128

    x = jnp.concatenate([x_prompt.reshape(mp, d), x_sample.reshape(ms, d)], axis=0)
    xb = x.astype(BF16)
    ff_pad = D_FF_PAD - D_FF

    p_c, p_n, p_m, p_s, p_conv = [], [], [], [], []
    s_c, s_n, s_m, s_s, s_conv = [], [], [], [], []
    for l in range(DEPTH):
        j = l // 2
        if l % 2 == 0:
            w_in = w_a_in[j]
            w_main = w_in[:, :A_MAIN].astype(BF16)
            w_gate = jnp.pad(w_in[:, A_MAIN:], ((0, 0), (0, LANES - 2 * NH_A))).astype(BF16)
            b_gate = jnp.pad(b_a_gate[j], (0, LANES - 2 * NH_A)).reshape(1, LANES)
            proj = matmul(xb, w_main, bm=544, bn=1024, name="mlstm_in_proj")
            gates = mlstm_gates(xb, w_gate, b_gate)
            y_p, c_p, n_p, m_p = mlstm_scan(proj, gates, a_norm_w[j], row0=0, n_seq=1,
                                            n_chunk=mp // l_prompt, L=l_prompt)
            y_s, c_s2, n_s2, m_s2 = mlstm_scan(
                proj, gates, a_norm_w[j], row0=mp, n_seq=bs, n_chunk=1, L=ts,
                init=(state_mlstm_C[j], state_mlstm_n[j], state_mlstm_m[j]))
            p_c.append(c_p); p_n.append(n_p); p_m.append(m_p)
            s_c.append(c_s2); s_n.append(n_s2); s_m.append(m_s2)
            y_mix = jnp.concatenate([y_p, y_s], axis=0)
            w_out = w_a_out[j].astype(BF16)
        else:
            w_in = w_b_in[j].astype(BF16)
            proj = matmul(xb, w_in, bm=544, bn=1024, name="hgrn_in_proj")
            y_mix, st = hgrn_scan(proj, b_lower, b_norm_w[j], state_hgrn_S[j], layer_j=j,
                                  n_prompt_chunks=mp // CHUNK)
            p_s.append(st[:1]); s_s.append(st[1:])
            w_out = w_b_out[j].astype(BF16)
        y = matmul(y_mix, w_out, bm=544, bn=1024, res=x, alpha=ALPHA, name="mixer_out_proj")
        x, xb = layer_norm(y, ln_g[l, 0], ln_b[l, 0])

        wg = jnp.pad(w_ffn_gate[l], ((0, 0), (0, ff_pad))).astype(BF16)
        wu = jnp.pad(w_ffn_up[l], ((0, 0), (0, ff_pad))).astype(BF16)
        wd = jnp.pad(w_ffn_down[l], ((0, ff_pad), (0, 0))).astype(BF16)
        cw = jnp.pad(ffn_conv_w[l], ((0, 0), (0, ff_pad)))
        cbias = jnp.pad(ffn_conv_b[l], (0, ff_pad)).reshape(1, D_FF_PAD)
        st8 = jnp.pad(state_ffn_conv[l], ((0, 0), (SUBLANES - (CONV_W - 1), 0), (0, ff_pad)))
        hmid, so = ffn_gate_up(xb, wg, wu, cw, cbias, st8, bm=ffn_bm, bn=512,
                               n_prompt_tiles=mp // ffn_bm, seq_len=ts)
        conv_new = so[:, SUBLANES - (CONV_W - 1):, :D_FF]
        p_conv.append(conv_new[:1]); s_conv.append(conv_new[1:])
        y = matmul(hmid, wd, bm=1088, bn=1024, bk=1408, res=x, alpha=ALPHA, name="ffn_down_proj")
        x, xb = layer_norm(y, ln_g[l, 1], ln_b[l, 1])

    y_prompt = x[:mp].reshape(bp, tp, d)
    y_sample = x[mp:].reshape(bs, ts, d)
    return (y_prompt, y_sample,
            jnp.stack(p_c), jnp.stack(p_n), jnp.stack(p_m), jnp.stack(p_s), jnp.stack(p_conv),
            jnp.stack(s_c), jnp.stack(s_n), jnp.stack(s_m), jnp.stack(s_s), jnp.stack(s_conv))
```

```python
import functools

import jax
import jax.numpy as jnp
from jax import lax
from jax.experimental import pallas as pl
from jax.experimental.pallas import tpu as pltpu

F32 = jnp.float32
BF16 = jnp.bfloat16

D_MODEL = 4096
DEPTH = 4
CHUNK = 64
NH_A = 8
DK_A = 256
DV_A = 512
QK_A = NH_A * DK_A
VD_A = NH_A * DV_A
A_MAIN = 2 * QK_A + 2 * VD_A
GATE_SOFTCAP = 15.0
NH_B = 32
DK_B = 128
DV_B = 128
D_FF = 11008
CONV_W = 3
ALPHA = (2 * DEPTH) ** 0.25
LN_EPS = 1e-5
HEAD_EPS = 1e-6

LANES = 128
SUBLANES = 8
VMEM_LIMIT_CAP = 58 * 1024 * 1024


def _vmem_limit(nbytes):
    return int(min(VMEM_LIMIT_CAP, nbytes * 1.2 + (6 << 20)))


def _nbytes(shape, dtype):
    n = 1
    for s in shape:
        n *= s
    return n * jnp.dtype(dtype).itemsize


def _mm_kernel(*refs, cast_w, has_res, alpha):
    if cast_w:
        wb_ref = refs[-1]
        refs = refs[:-1]
    if has_res:
        x_ref, w_ref, r_ref, o_ref = refs
    else:
        x_ref, w_ref, o_ref = refs
    if cast_w:
        @pl.when(pl.program_id(1) == 0)
        def _():
            wb_ref[...] = w_ref[...].astype(BF16)
        w = wb_ref[...]
    else:
        w = w_ref[...]
    acc = jnp.dot(x_ref[...], w, preferred_element_type=F32)
    if has_res:
        acc = alpha * r_ref[...] + acc
    o_ref[...] = acc.astype(o_ref.dtype)


def matmul(x, w, *, bm, bn, n_cols=None, res=None, alpha=1.0, out_dtype=F32, w_resident=True, name="mm"):
    m, kdim = x.shape
    n = w.shape[1] if n_cols is None else n_cols
    cast_w = w.dtype != BF16
    assert m % bm == 0 and n % bn == 0 and (w_resident or not cast_w)
    if w_resident:
        grid = (n // bn, m // bm)
        xi = lambda j, i: (i, 0)
        wi = lambda j, i: (0, j)
        oi = lambda j, i: (i, j)
    else:
        grid = (m // bm, n // bn)
        xi = lambda i, j: (i, 0)
        wi = lambda i, j: (0, j)
        oi = lambda i, j: (i, j)
    in_specs = [pl.BlockSpec((bm, kdim), xi), pl.BlockSpec((kdim, bn), wi)]
    args = [x, w]
    est = 2 * (_nbytes((bm, kdim), x.dtype) + _nbytes((kdim, bn), w.dtype) + _nbytes((bm, bn), out_dtype))
    est += _nbytes((bm, bn), F32)
    if res is not None:
        in_specs.append(pl.BlockSpec((bm, bn), oi))
        args.append(res)
        est += 2 * _nbytes((bm, bn), res.dtype)
    scratch = []
    if cast_w:
        scratch.append(pltpu.VMEM((kdim, bn), BF16))
        est += _nbytes((kdim, bn), BF16)
    return pl.pallas_call(
        functools.partial(_mm_kernel, cast_w=cast_w, has_res=res is not None, alpha=alpha),
        grid=grid,
        in_specs=in_specs,
        out_specs=pl.BlockSpec((bm, bn), oi),
        out_shape=jax.ShapeDtypeStruct((m, n), out_dtype),
        scratch_shapes=scratch,
        compiler_params=pltpu.CompilerParams(
            dimension_semantics=("arbitrary", "arbitrary"), vmem_limit_bytes=_vmem_limit(est)),
        name=name,
    )(*args)


def _ln_kernel(y_ref, g_ref, b_ref, o_ref, ob_ref):
    y = y_ref[...]
    mu = jnp.mean(y, axis=-1, keepdims=True)
    yc = y - mu
    var = jnp.mean(yc * yc, axis=-1, keepdims=True)
    out = yc * lax.rsqrt(var + LN_EPS) * g_ref[...] + b_ref[...]
    o_ref[...] = out
    ob_ref[...] = out.astype(BF16)


def layer_norm(y, g, b, *, bm=256):
    m, d = y.shape
    row = pl.BlockSpec((bm, d), lambda i: (i, 0))
    vec = pl.BlockSpec((1, d), lambda i: (0, 0))
    est = 2 * (2 * _nbytes((bm, d), F32) + _nbytes((bm, d), BF16)) + 2 * _nbytes((bm, d), F32)
    return pl.pallas_call(
        _ln_kernel,
        grid=(m // bm,),
        in_specs=[row, vec, vec],
        out_specs=[row, row],
        out_shape=[jax.ShapeDtypeStruct((m, d), F32), jax.ShapeDtypeStruct((m, d), BF16)],
        compiler_params=pltpu.CompilerParams(
            dimension_semantics=("arbitrary",), vmem_limit_bytes=_vmem_limit(est)),
        name="layer_norm",
    )(y, g.reshape(1, d), b.reshape(1, d))


def _log_sigmoid(z):
    return jnp.minimum(z, 0.0) - jnp.log1p(jnp.exp(-jnp.abs(z)))


def _gates_kernel(x_ref, w_ref, b_ref, o_ref):
    g = jnp.dot(x_ref[...], w_ref[...], preferred_element_type=F32) + b_ref[...]
    sc = GATE_SOFTCAP * jnp.tanh(g / GATE_SOFTCAP)
    lane = lax.broadcasted_iota(jnp.int32, sc.shape, 1)
    o_ref[...] = jnp.where(lane < NH_A, sc, _log_sigmoid(sc))


def mlstm_gates(xb, w_gate, b_gate, *, bm=1088):
    m, d = xb.shape
    est = 2 * (_nbytes((bm, d), BF16) + _nbytes((d, LANES), BF16) + _nbytes((bm, LANES), F32))
    return pl.pallas_call(
        _gates_kernel,
        grid=(m // bm,),
        in_specs=[pl.BlockSpec((bm, d), lambda i: (i, 0)),
                  pl.BlockSpec((d, LANES), lambda i: (0, 0)),
                  pl.BlockSpec((1, LANES), lambda i: (0, 0))],
        out_specs=pl.BlockSpec((bm, LANES), lambda i: (i, 0)),
        out_shape=jax.ShapeDtypeStruct((m, LANES), F32),
        compiler_params=pltpu.CompilerParams(
            dimension_semantics=("arbitrary",), vmem_limit_bytes=_vmem_limit(est)),
        name="mlstm_gates",
    )(xb, w_gate, b_gate)


def _mlstm_kernel(*refs, L, has_init):
    if has_init:
        (q_ref, k_ref, v_ref, o_ref, gc_ref, gr_ref, nw_ref, c0_ref, n0_ref, m0_ref,
         y_ref, ct_ref, nt_ref, mt_ref, c_s, n_s, m_s) = refs
    else:
        (q_ref, k_ref, v_ref, o_ref, gc_ref, gr_ref, nw_ref,
         y_ref, ct_ref, nt_ref, mt_ref, c_s, n_s, m_s) = refs
    c = pl.program_id(1)
    nc = pl.num_programs(1)

    @pl.when(c == 0)
    def _():
        if has_init:
            c_s[...] = c0_ref[0]
            n_s[...] = n0_ref[0]
            m_s[...] = m0_ref[0]
        else:
            c_s[...] = jnp.zeros_like(c_s)
            n_s[...] = jnp.zeros_like(n_s)
            m_s[...] = jnp.zeros_like(m_s)

    t_idx = lax.broadcasted_iota(jnp.int32, (L, L), 0)
    s_idx = lax.broadcasted_iota(jnp.int32, (L, L), 1)
    causal = s_idx <= t_idx
    lane = lax.broadcasted_iota(jnp.int32, (L, LANES), 1)
    gcol = gc_ref[...]

    def head(h, carry):
        qs = pl.ds(pl.multiple_of(h * DK_A, DK_A), DK_A)
        vs = pl.ds(pl.multiple_of(h * DV_A, DV_A), DV_A)
        q = q_ref[:, qs]
        k = k_ref[:, qs] * (DK_A ** -0.5)
        v = v_ref[:, vs]
        qb = q.astype(BF16)
        kb = k.astype(BF16)
        vb = v.astype(BF16)
        ig_r = gr_ref[h, 0]
        lf_r = gr_ref[NH_A + h, 0]
        ig_c = jnp.sum(jnp.where(lane == h, gcol, 0.0), axis=1, keepdims=True)
        lf_c = jnp.sum(jnp.where(lane == NH_A + h, gcol, 0.0), axis=1, keepdims=True)
        f_c = jnp.sum(jnp.where(causal, lf_r, 0.0), axis=1, keepdims=True)
        f_r = jnp.sum(jnp.where(t_idx <= s_idx, lf_c, 0.0), axis=0, keepdims=True)
        f_end = jnp.sum(lf_r, axis=1, keepdims=True)
        m_prev = m_s[h]
        c_prev = c_s[h]
        n_prev = n_s[h]

        d = jnp.where(causal, f_c - f_r + ig_r, -jnp.inf)
        g = f_c + m_prev
        m_t = jnp.maximum(g, jnp.max(d, axis=1, keepdims=True))
        w = jnp.exp(d - m_t)
        inter = jnp.exp(g - m_t)
        s = lax.dot_general(qb, kb, (((1,), (1,)), ((), ())), preferred_element_type=F32) * w
        num = jnp.dot(s.astype(BF16), vb, preferred_element_type=F32)
        num = num + inter * jnp.dot(qb, c_prev.astype(BF16), preferred_element_type=F32)
        den = jnp.sum(s, axis=1, keepdims=True) + inter * jnp.sum(q * n_prev, axis=1, keepdims=True)
        hh = num / jnp.maximum(jnp.abs(den), jnp.exp(-m_t))

        d_end_r = f_end - f_r + ig_r
        d_end_c = f_end - f_c + ig_c
        m_new = jnp.maximum(f_end + m_prev, jnp.max(d_end_r, axis=1, keepdims=True))
        w_end_c = jnp.exp(d_end_c - m_new)
        decay = jnp.exp(f_end + m_prev - m_new)
        kw = k * w_end_c
        c_s[h] = decay * c_prev + lax.dot_general(
            kw.astype(BF16), vb, (((0,), (0,)), ((), ())), preferred_element_type=F32)
        n_s[h] = decay * n_prev + jnp.sum(kw, axis=0, keepdims=True)
        m_s[h] = m_new

        mu = jnp.mean(hh, axis=1, keepdims=True)
        hc = hh - mu
        hn = hc * lax.rsqrt(jnp.mean(hc * hc, axis=1, keepdims=True) + HEAD_EPS)
        hn = hn * nw_ref[:, vs]
        y_ref[:, vs] = (jax.nn.sigmoid(o_ref[:, vs]) * hn).astype(BF16)
        return carry

    lax.fori_loop(0, NH_A, head, 0)

    @pl.when(c == nc - 1)
    def _():
        ct_ref[0] = c_s[...]
        nt_ref[0] = n_s[...]
        mt_ref[0] = m_s[...]


def mlstm_scan(proj, gates, norm_w, *, row0, n_seq, n_chunk, L, init=None):
    rb0 = row0 // L
    rows = n_seq * n_chunk * L
    g_rows = lax.slice_in_dim(gates, row0, row0 + rows, axis=0)
    g_t = g_rows[:, :2 * NH_A].T.reshape(2 * NH_A, n_seq * n_chunk, 1, L)
    rmap = lambda s, c: (rb0 + s * n_chunk + c)
    in_specs = [
        pl.BlockSpec((L, QK_A), lambda s, c: (rmap(s, c), 0)),
        pl.BlockSpec((L, QK_A), lambda s, c: (rmap(s, c), 1)),
        pl.BlockSpec((L, VD_A), lambda s, c: (rmap(s, c), 1)),
        pl.BlockSpec((L, VD_A), lambda s, c: (rmap(s, c), 2)),
        pl.BlockSpec((L, LANES), lambda s, c: (rmap(s, c), 0)),
        pl.BlockSpec((2 * NH_A, 1, 1, L), lambda s, c: (0, s * n_chunk + c, 0, 0)),
        pl.BlockSpec((1, VD_A), lambda s, c: (0, 0)),
    ]
    args = [proj, proj, proj, proj, gates, g_t, norm_w.reshape(1, VD_A)]
    state_specs = [
        pl.BlockSpec((1, NH_A, DK_A, DV_A), lambda s, c: (s, 0, 0, 0)),
        pl.BlockSpec((1, NH_A, 1, DK_A), lambda s, c: (s, 0, 0, 0)),
        pl.BlockSpec((1, NH_A, 1, 1), lambda s, c: (s, 0, 0, 0)),
    ]
    if init is not None:
        c0, n0, m0 = init
        in_specs += state_specs
        args += [c0, n0.reshape(n_seq, NH_A, 1, DK_A), m0.reshape(n_seq, NH_A, 1, 1)]
    est = 2 * (2 * _nbytes((L, QK_A), F32) + 2 * _nbytes((L, VD_A), F32) + _nbytes((L, VD_A), BF16))
    est += (3 if init is None else 5) * _nbytes((NH_A, DK_A, DV_A), F32)
    est += 8 * _nbytes((L, max(L, DV_A)), F32)
    y, ct, nt, mt = pl.pallas_call(
        functools.partial(_mlstm_kernel, L=L, has_init=init is not None),
        grid=(n_seq, n_chunk),
        in_specs=in_specs,
        out_specs=[pl.BlockSpec((L, VD_A), lambda s, c: (s * n_chunk + c, 0))] + state_specs,
        out_shape=[
            jax.ShapeDtypeStruct((rows, VD_A), BF16),
            jax.ShapeDtypeStruct((n_seq, NH_A, DK_A, DV_A), F32),
            jax.ShapeDtypeStruct((n_seq, NH_A, 1, DK_A), F32),
            jax.ShapeDtypeStruct((n_seq, NH_A, 1, 1), F32),
        ],
        scratch_shapes=[
            pltpu.VMEM((NH_A, DK_A, DV_A), F32),
            pltpu.VMEM((NH_A, 1, DK_A), F32),
            pltpu.VMEM((NH_A, 1, 1), F32),
        ],
        compiler_params=pltpu.CompilerParams(
            dimension_semantics=("arbitrary", "arbitrary"), vmem_limit_bytes=_vmem_limit(est)),
        name="mlstm_scan_init" if init is not None else "mlstm_scan",
    )(*args)
    return y, ct, nt.reshape(n_seq, NH_A, DK_A), mt.reshape(n_seq, NH_A)


def _hgrn_kernel(q_ref, f_ref, i_ref, g_ref, bl_ref, nw_ref, s0_ref, y_ref, st_ref, s_s,
                 *, layer_j, n_prompt_chunks, head_unroll):
    L = CHUNK
    g_id = pl.program_id(0)

    @pl.when(g_id == 0)
    def _():
        s_s[...] = jnp.zeros_like(s_s)

    @pl.when(g_id >= n_prompt_chunks)
    def _():
        def load(h, carry):
            s_s[h] = s0_ref[0, h].T
            return carry
        lax.fori_loop(0, NH_B, load, 0)

    row = lax.broadcasted_iota(jnp.int32, (L, DK_B), 0)
    t_idx = lax.broadcasted_iota(jnp.int32, (L, L), 0)
    s_idx = lax.broadcasted_iota(jnp.int32, (L, L), 1)
    levels = [1, 2, 4, 8, 16, 32]
    upper = {mm: (row & mm) != 0 for mm in levels}
    pair = {mm: ((t_idx ^ s_idx) < 2 * mm) & ((t_idx & mm) != 0) & ((s_idx & mm) == 0) for mm in levels}
    diag = t_idx == s_idx
    nt_dims = (((1,), (1,)), ((), ()))
    tn_dims = (((0,), (0,)), ((), ()))

    def head(h, carry):
        sl = pl.ds(pl.multiple_of(h * DK_B, DK_B), DK_B)
        q = q_ref[:, sl]
        fz = f_ref[:, sl]
        inp = i_ref[:, sl]
        gate = g_ref[:, sl]
        bl = bl_ref[:, sl]
        e = jnp.exp(bl - jnp.max(bl, axis=0, keepdims=True))
        p = e / jnp.sum(e, axis=0, keepdims=True)
        lb = jnp.zeros((1, DK_B), F32)
        for r in range(1, layer_j + 1):
            lb = lb + p[r:r + 1]
        sig = jax.nn.sigmoid(fz)
        oml = 1.0 - lb
        logf = jnp.log(lb + oml * sig)
        k = oml * (1.0 - sig)

        ib = inp.astype(BF16)
        s_acc = jnp.where(
            diag, lax.dot_general(q.astype(BF16), k.astype(BF16), nt_dims, preferred_element_type=F32), 0.0)
        p_sum = logf
        r_sum = jnp.zeros_like(logf)
        g_sum = logf
        for mm in levels:
            up = upper[mm]
            sib = jnp.where(up, pltpu.roll(g_sum, mm, 0), pltpu.roll(g_sum, L - mm, 0))
            x = (jnp.where(up, q, k) * jnp.exp(jnp.where(up, p_sum, r_sum))).astype(BF16)
            s_m = lax.dot_general(x, x, nt_dims, preferred_element_type=F32)
            s_acc = jnp.where(pair[mm], s_m, s_acc)
            p_sum = p_sum + jnp.where(up, sib, 0.0)
            r_sum = r_sum + jnp.where(up, 0.0, sib)
            g_sum = g_sum + sib
        s_prev = s_s[h]
        o = lax.dot_general((q * jnp.exp(p_sum)).astype(BF16), s_prev.astype(BF16), nt_dims,
                            preferred_element_type=F32)
        o = o + jnp.dot(s_acc.astype(BF16), ib, preferred_element_type=F32)
        k_end = (k * jnp.exp(r_sum)).astype(BF16)
        s_s[h] = jnp.exp(g_sum[0:1]) * s_prev + lax.dot_general(ib, k_end, tn_dims, preferred_element_type=F32)

        o = o * lax.rsqrt(jnp.mean(o * o, axis=1, keepdims=True) + HEAD_EPS)
        o = o * nw_ref[:, sl] * (gate * jax.nn.sigmoid(gate))
        y_ref[:, sl] = o.astype(BF16)
        return carry

    lax.fori_loop(0, NH_B, head, 0, unroll=head_unroll)

    @pl.when(g_id >= n_prompt_chunks - 1)
    def _():
        def store(h, carry):
            st_ref[0, h] = s_s[h].T
            return carry
        lax.fori_loop(0, NH_B, store, 0)


def hgrn_scan(proj, b_lower, norm_w, s0, *, layer_j, n_prompt_chunks, head_unroll=8):
    m = proj.shape[0]
    n_chunks = m // CHUNK
    n_seq = n_chunks - n_prompt_chunks + 1
    n_b = b_lower.shape[0]
    col = lambda cb: pl.BlockSpec((CHUNK, D_MODEL), lambda g: (g, cb))
    est = 2 * (4 * _nbytes((CHUNK, D_MODEL), F32) + _nbytes((CHUNK, D_MODEL), BF16))
    est += 5 * _nbytes((NH_B, DK_B, DV_B), F32)
    y, st = pl.pallas_call(
        functools.partial(_hgrn_kernel, layer_j=layer_j, n_prompt_chunks=n_prompt_chunks,
                          head_unroll=head_unroll),
        grid=(n_chunks,),
        in_specs=[col(0), col(1), col(2), col(3),
                  pl.BlockSpec((n_b, D_MODEL), lambda g: (0, 0)),
                  pl.BlockSpec((1, D_MODEL), lambda g: (0, 0)),
                  pl.BlockSpec((1, NH_B, DK_B, DV_B),
                               lambda g: (jnp.maximum(g - n_prompt_chunks, 0), 0, 0, 0))],
        out_specs=[pl.BlockSpec((CHUNK, D_MODEL), lambda g: (g, 0)),
                   pl.BlockSpec((1, NH_B, DK_B, DV_B),
                                lambda g: (jnp.maximum(g - (n_prompt_chunks - 1), 0), 0, 0, 0))],
        out_shape=[jax.ShapeDtypeStruct((m, D_MODEL), BF16),
                   jax.ShapeDtypeStruct((n_seq, NH_B, DK_B, DV_B), F32)],
        scratch_shapes=[pltpu.VMEM((NH_B, DV_B, DK_B), F32)],
        compiler_params=pltpu.CompilerParams(
            dimension_semantics=("arbitrary",), vmem_limit_bytes=_vmem_limit(est)),
        name="hgrn_scan",
    )(proj, proj, proj, proj, b_lower, norm_w.reshape(1, D_MODEL), s0)
    return y, st


def _ffn_kernel(x_ref, wg_ref, wu_ref, cw_ref, cb_ref, st_ref, h_ref, so_ref, carry_ref, wgb_ref, wub_ref,
                *, bm, n_prompt_tiles, seq_len):
    i = pl.program_id(1)

    @pl.when(i == 0)
    def _():
        wgb_ref[...] = wg_ref[...].astype(BF16)
        wub_ref[...] = wu_ref[...].astype(BF16)
        carry_ref[...] = jnp.zeros_like(carry_ref)

    x = x_ref[...]
    a = jnp.dot(x, wgb_ref[...], preferred_element_type=F32)
    u = jnp.dot(x, wub_ref[...], preferred_element_type=F32)
    w0 = cw_ref[0:1, :]
    w1 = cw_ref[1:2, :]
    w2 = cw_ref[2:3, :]
    cb = cb_ref[...]

    def act(a_cur, a_p1, a_p2, u_cur):
        c = cb + a_p2 * w0
        c = c + a_p1 * w1
        c = c + a_cur * w2
        return (c * jax.nn.sigmoid(c) * u_cur).astype(BF16)

    h_ref[...] = act(a, pltpu.roll(a, 1, 0), pltpu.roll(a, 2, 0), u)

    rows8 = lax.broadcasted_iota(jnp.int32, (SUBLANES, a.shape[1]), 0)

    def patch(r0, prev8):
        blk = a[r0:r0 + SUBLANES]
        p1 = jnp.where(rows8 < 1, pltpu.roll(prev8, 1, 0), pltpu.roll(blk, 1, 0))
        p2 = jnp.where(rows8 < 2, pltpu.roll(prev8, 2, 0), pltpu.roll(blk, 2, 0))
        h_ref[r0:r0 + SUBLANES, :] = act(blk, p1, p2, u[r0:r0 + SUBLANES])

    is_sample = i >= n_prompt_tiles
    patch(0, jnp.where(is_sample, st_ref[0], carry_ref[...]))
    n_groups = bm // seq_len

    @pl.when(is_sample)
    def _():
        for s in range(1, n_groups):
            patch(s * seq_len, st_ref[s])
        for s in range(n_groups):
            so_ref[1 + s] = a[(s + 1) * seq_len - SUBLANES:(s + 1) * seq_len]

    carry_ref[...] = a[bm - SUBLANES:bm]

    @pl.when(i == n_prompt_tiles - 1)
    def _():
        so_ref[0] = a[bm - SUBLANES:bm]


def ffn_gate_up(xb, wg, wu, conv_w, conv_b, state8, *, bm, bn, n_prompt_tiles, seq_len):
    m, d = xb.shape
    n = wg.shape[1]
    assert m % bm == 0 and n % bn == 0
    n_seq_out = state8.shape[0] + 1
    est = 2 * (_nbytes((bm, d), BF16) + 2 * _nbytes((d, bn), wg.dtype) + _nbytes((bm, bn), BF16))
    est += 2 * _nbytes((d, bn), BF16)
    est += 4 * _nbytes((bm, bn), F32) + 4 * _nbytes((n_seq_out, SUBLANES, bn), F32)
    return pl.pallas_call(
        functools.partial(_ffn_kernel, bm=bm, n_prompt_tiles=n_prompt_tiles, seq_len=seq_len),
        grid=(n // bn, m // bm),
        in_specs=[pl.BlockSpec((bm, d), lambda j, i: (i, 0)),
                  pl.BlockSpec((d, bn), lambda j, i: (0, j)),
                  pl.BlockSpec((d, bn), lambda j, i: (0, j)),
                  pl.BlockSpec((CONV_W, bn), lambda j, i: (0, j)),
                  pl.BlockSpec((1, bn), lambda j, i: (0, j)),
                  pl.BlockSpec((state8.shape[0], SUBLANES, bn), lambda j, i: (0, 0, j))],
        out_specs=[pl.BlockSpec((bm, bn), lambda j, i: (i, j)),
                   pl.BlockSpec((n_seq_out, SUBLANES, bn), lambda j, i: (0, 0, j))],
        out_shape=[jax.ShapeDtypeStruct((m, n), BF16),
                   jax.ShapeDtypeStruct((n_seq_out, SUBLANES, n), F32)],
        scratch_shapes=[pltpu.VMEM((SUBLANES, bn), F32),
                        pltpu.VMEM((d, bn), BF16),
                        pltpu.VMEM((d, bn), BF16)],
        compiler_params=pltpu.CompilerParams(
            dimension_semantics=("arbitrary", "arbitrary"), vmem_limit_bytes=_vmem_limit(est)),
        name="ffn_gate_up",
    )(xb, wg, wu, conv_w, conv_b, state8)


def kernel(x_prompt, x_sample, state_mlstm_C, state_mlstm_n, state_mlstm_m, state_hgrn_S, state_ffn_conv,
           ln_g, ln_b, w_a_in, b_a_gate, a_norm_w, w_a_out, w_b_in, b_lower, b_norm_w, w_b_out,
           w_ffn_gate, w_ffn_up, w_ffn_down, ffn_conv_w, ffn_conv_b):
    bp, tp, d = x_prompt.shape
    bs, ts, _ = x_sample.shape
    mp = bp * tp
    ms = bs * ts
    m = mp + ms
    assert bp == 1 and ts == CHUNK and d == D_MODEL
    ffn_bm = ms
    assert mp % ffn_bm == 0
    l_prompt = 128

    x = jnp.concatenate([x_prompt.reshape(mp, d), x_sample.reshape(ms, d)], axis=0)
    xb = x.astype(BF16)

    p_c, p_n, p_m, p_s, p_conv = [], [], [], [], []
    s_c, s_n, s_m, s_s, s_conv = [], [], [], [], []
    for l in range(DEPTH):
        j = l // 2
        if l % 2 == 0:
            w_in = w_a_in[j]
            w_gate = jnp.pad(w_in[:, A_MAIN:], ((0, 0), (0, LANES - 2 * NH_A))).astype(BF16)
            b_gate = jnp.pad(b_a_gate[j], (0, LANES - 2 * NH_A)).reshape(1, LANES)
            proj = matmul(xb, w_in, n_cols=A_MAIN, bm=1088, bn=512, name="mlstm_in_proj")
            gates = mlstm_gates(xb, w_gate, b_gate)
            y_p, c_p, n_p, m_p = mlstm_scan(proj, gates, a_norm_w[j], row0=0, n_seq=1,
                                            n_chunk=mp // l_prompt, L=l_prompt)
            y_s, c_s2, n_s2, m_s2 = mlstm_scan(
                proj, gates, a_norm_w[j], row0=mp, n_seq=bs, n_chunk=1, L=ts,
                init=(state_mlstm_C[j], state_mlstm_n[j], state_mlstm_m[j]))
            p_c.append(c_p); p_n.append(n_p); p_m.append(m_p)
            s_c.append(c_s2); s_n.append(n_s2); s_m.append(m_s2)
            y_mix = jnp.concatenate([y_p, y_s], axis=0)
            w_out = w_a_out[j]
        else:
            proj = matmul(xb, w_b_in[j], bm=1088, bn=512, name="hgrn_in_proj")
            y_mix, st = hgrn_scan(proj, b_lower, b_norm_w[j], state_hgrn_S[j], layer_j=j,
                                  n_prompt_chunks=mp // CHUNK)
            p_s.append(st[:1]); s_s.append(st[1:])
            w_out = w_b_out[j]
        y = matmul(y_mix, w_out, bm=544, bn=512, res=x, alpha=ALPHA, name="mixer_out_proj")
        x, xb = layer_norm(y, ln_g[l, 0], ln_b[l, 0])

        st8 = jnp.pad(state_ffn_conv[l], ((0, 0), (SUBLANES - (CONV_W - 1), 0), (0, 0)))
        hmid, so = ffn_gate_up(xb, w_ffn_gate[l], w_ffn_up[l], ffn_conv_w[l], ffn_conv_b[l].reshape(1, D_FF),
                               st8, bm=ffn_bm, bn=256, n_prompt_tiles=mp // ffn_bm, seq_len=ts)
        conv_new = so[:, SUBLANES - (CONV_W - 1):, :]
        p_conv.append(conv_new[:1]); s_conv.append(conv_new[1:])
        y = matmul(hmid, w_ffn_down[l].astype(BF16), bm=544, bn=256, res=x, alpha=ALPHA,
                   w_resident=False, name="ffn_down_proj")
        x, xb = layer_norm(y, ln_g[l, 1], ln_b[l, 1])

    y_prompt = x[:mp].reshape(bp, tp, d)
    y_sample = x[mp:].reshape(bs, ts, d)
    return (y_prompt, y_sample,
            jnp.stack(p_c), jnp.stack(p_n), jnp.stack(p_m), jnp.stack(p_s), jnp.stack(p_conv),
            jnp.stack(s_c), jnp.stack(s_n), jnp.stack(s_m), jnp.stack(s_s), jnp.stack(s_conv))
```

```python
import functools

import jax
import jax.numpy as jnp
from jax import lax
from jax.experimental import pallas as pl
from jax.experimental.pallas import tpu as pltpu

F32 = jnp.float32
BF16 = jnp.bfloat16

D_MODEL = 4096
DEPTH = 4
CHUNK = 64
NH_A = 8
DK_A = 256
DV_A = 512
QK_A = NH_A * DK_A
VD_A = NH_A * DV_A
A_MAIN = 2 * QK_A + 2 * VD_A
GATE_SOFTCAP = 15.0
NH_B = 32
DK_B = 128
DV_B = 128
D_FF = 11008
CONV_W = 3
ALPHA = (2 * DEPTH) ** 0.25
LN_EPS = 1e-5
HEAD_EPS = 1e-6

LANES = 128
SUBLANES = 8
VMEM_LIMIT_CAP = 58 * 1024 * 1024


def _vmem_limit(nbytes):
    return int(min(VMEM_LIMIT_CAP, nbytes * 1.2 + (6 << 20)))


def _nbytes(shape, dtype):
    n = 1
    for s in shape:
        n *= s
    return n * jnp.dtype(dtype).itemsize


def _finish(acc, r_ref, o_ref, alpha):
    if r_ref is not None:
        acc = alpha * r_ref[...] + acc
    o_ref[...] = acc.astype(o_ref.dtype)


def _mm_wcast_kernel(*refs, nn, nm, has_res, alpha):
    if has_res:
        x_ref, wc_ref, r_ref, o_ref, wb_ref = refs
    else:
        x_ref, wc_ref, o_ref, wb_ref = refs
        r_ref = None
    p = pl.program_id(0)
    i = pl.program_id(1)
    crows = wc_ref.shape[0]

    @pl.when(p < nn)
    def _():
        wb_ref[p % 2, pl.ds(pl.multiple_of(i * crows, crows), crows), :] = wc_ref[...].astype(BF16)

    @pl.when(p > 0)
    def _():
        acc = jnp.dot(x_ref[...], wb_ref[(p + 1) % 2], preferred_element_type=F32)
        _finish(acc, r_ref, o_ref, alpha)


def matmul_wcast(x, w, layer, *, bm, bn, n_cols=None, res=None, alpha=1.0, out_dtype=F32, name="mm"):
    m, kdim = x.shape
    n = w.shape[2] if n_cols is None else n_cols
    nm, nn = m // bm, n // bn
    crows = kdim // nm
    assert m % bm == 0 and n % bn == 0 and kdim % nm == 0 and crows % SUBLANES == 0
    row = lambda p, i: jnp.where(p > 0, i, 0)
    xi = lambda p, i: (row(p, i), 0)
    oi = lambda p, i: (row(p, i), jnp.maximum(p - 1, 0))
    wi = lambda p, i: (layer, jnp.where(p < nn, i, nm - 1), jnp.minimum(p, nn - 1))
    in_specs = [pl.BlockSpec((bm, kdim), xi), pl.BlockSpec((None, crows, bn), wi)]
    args = [x, w]
    est = 2 * (_nbytes((bm, kdim), BF16) + _nbytes((crows, bn), F32) + _nbytes((bm, bn), out_dtype))
    est += _nbytes((bm, bn), F32) + 2 * _nbytes((kdim, bn), BF16)
    if res is not None:
        in_specs.append(pl.BlockSpec((bm, bn), oi))
        args.append(res)
        est += 2 * _nbytes((bm, bn), res.dtype)
    return pl.pallas_call(
        functools.partial(_mm_wcast_kernel, nn=nn, nm=nm, has_res=res is not None, alpha=alpha),
        grid=(nn + 1, nm),
        in_specs=in_specs,
        out_specs=pl.BlockSpec((bm, bn), oi),
        out_shape=jax.ShapeDtypeStruct((m, n), out_dtype),
        scratch_shapes=[pltpu.VMEM((2, kdim, bn), BF16)],
        compiler_params=pltpu.CompilerParams(
            dimension_semantics=("arbitrary", "arbitrary"), vmem_limit_bytes=_vmem_limit(est)),
        name=name,
    )(*args)


def _mm_kernel(x_ref, w_ref, r_ref, o_ref, *, alpha):
    _finish(jnp.dot(x_ref[...], w_ref[...], preferred_element_type=F32), r_ref, o_ref, alpha)


def matmul_xres(x, w, res, *, bm, bn, alpha, name="mm"):
    m, kdim = x.shape
    n = w.shape[1]
    assert m % bm == 0 and n % bn == 0
    oi = lambda i, j: (i, j)
    est = 2 * (_nbytes((bm, kdim), BF16) + _nbytes((kdim, bn), BF16) + 2 * _nbytes((bm, bn), F32))
    est += _nbytes((bm, bn), F32)
    return pl.pallas_call(
        functools.partial(_mm_kernel, alpha=alpha),
        grid=(m // bm, n // bn),
        in_specs=[pl.BlockSpec((bm, kdim), lambda i, j: (i, 0)),
                  pl.BlockSpec((kdim, bn), lambda i, j: (0, j)),
                  pl.BlockSpec((bm, bn), oi)],
        out_specs=pl.BlockSpec((bm, bn), oi),
        out_shape=jax.ShapeDtypeStruct((m, n), F32),
        compiler_params=pltpu.CompilerParams(
            dimension_semantics=("arbitrary", "arbitrary"), vmem_limit_bytes=_vmem_limit(est)),
        name=name,
    )(x, w, res)


def _ln_kernel(y_ref, g_ref, b_ref, o_ref, ob_ref):
    y = y_ref[...]
    mu = jnp.mean(y, axis=-1, keepdims=True)
    yc = y - mu
    var = jnp.mean(yc * yc, axis=-1, keepdims=True)
    out = yc * lax.rsqrt(var + LN_EPS) * g_ref[...] + b_ref[...]
    o_ref[...] = out
    ob_ref[...] = out.astype(BF16)


def layer_norm(y, g, b, *, bm=256):
    m, d = y.shape
    row = pl.BlockSpec((bm, d), lambda i: (i, 0))
    vec = pl.BlockSpec((1, d), lambda i: (0, 0))
    est = 2 * (2 * _nbytes((bm, d), F32) + _nbytes((bm, d), BF16)) + 2 * _nbytes((bm, d), F32)
    return pl.pallas_call(
        _ln_kernel,
        grid=(m // bm,),
        in_specs=[row, vec, vec],
        out_specs=[row, row],
        out_shape=[jax.ShapeDtypeStruct((m, d), F32), jax.ShapeDtypeStruct((m, d), BF16)],
        compiler_params=pltpu.CompilerParams(
            dimension_semantics=("arbitrary",), vmem_limit_bytes=_vmem_limit(est)),
        name="layer_norm",
    )(y, g.reshape(1, d), b.reshape(1, d))


def _log_sigmoid(z):
    return jnp.minimum(z, 0.0) - jnp.log1p(jnp.exp(-jnp.abs(z)))


def _gates_kernel(x_ref, w_ref, b_ref, o_ref):
    g = jnp.dot(x_ref[...], w_ref[...], preferred_element_type=F32) + b_ref[...]
    sc = GATE_SOFTCAP * jnp.tanh(g / GATE_SOFTCAP)
    lane = lax.broadcasted_iota(jnp.int32, sc.shape, 1)
    o_ref[...] = jnp.where(lane < NH_A, sc, _log_sigmoid(sc))


def mlstm_gates(xb, w_gate, b_gate, *, bm=1088):
    m, d = xb.shape
    est = 2 * (_nbytes((bm, d), BF16) + _nbytes((d, LANES), BF16) + _nbytes((bm, LANES), F32))
    return pl.pallas_call(
        _gates_kernel,
        grid=(m // bm,),
        in_specs=[pl.BlockSpec((bm, d), lambda i: (i, 0)),
                  pl.BlockSpec((d, LANES), lambda i: (0, 0)),
                  pl.BlockSpec((1, LANES), lambda i: (0, 0))],
        out_specs=pl.BlockSpec((bm, LANES), lambda i: (i, 0)),
        out_shape=jax.ShapeDtypeStruct((m, LANES), F32),
        compiler_params=pltpu.CompilerParams(
            dimension_semantics=("arbitrary",), vmem_limit_bytes=_vmem_limit(est)),
        name="mlstm_gates",
    )(xb, w_gate, b_gate)


def _mlstm_kernel(*refs, L, has_init):
    if has_init:
        (q_ref, k_ref, v_ref, o_ref, gc_ref, gr_ref, nw_ref, c0_ref, n0_ref, m0_ref,
         y_ref, ct_ref, nt_ref, mt_ref, c_s, n_s, m_s) = refs
    else:
        (q_ref, k_ref, v_ref, o_ref, gc_ref, gr_ref, nw_ref,
         y_ref, ct_ref, nt_ref, mt_ref, c_s, n_s, m_s) = refs
    c = pl.program_id(1)
    nc = pl.num_programs(1)

    @pl.when(c == 0)
    def _():
        if has_init:
            c_s[...] = c0_ref[0]
            n_s[...] = n0_ref[0]
            m_s[...] = m0_ref[0]
        else:
            c_s[...] = jnp.zeros_like(c_s)
            n_s[...] = jnp.zeros_like(n_s)
            m_s[...] = jnp.zeros_like(m_s)

    t_idx = lax.broadcasted_iota(jnp.int32, (L, L), 0)
    s_idx = lax.broadcasted_iota(jnp.int32, (L, L), 1)
    causal = s_idx <= t_idx
    lane = lax.broadcasted_iota(jnp.int32, (L, LANES), 1)
    gcol = gc_ref[...]

    def head(h, carry):
        qs = pl.ds(pl.multiple_of(h * DK_A, DK_A), DK_A)
        vs = pl.ds(pl.multiple_of(h * DV_A, DV_A), DV_A)
        q = q_ref[:, qs]
        k = k_ref[:, qs] * (DK_A ** -0.5)
        v = v_ref[:, vs]
        qb = q.astype(BF16)
        kb = k.astype(BF16)
        vb = v.astype(BF16)
        ig_r = gr_ref[h, 0]
        lf_r = gr_ref[NH_A + h, 0]
        ig_c = jnp.sum(jnp.where(lane == h, gcol, 0.0), axis=1, keepdims=True)
        lf_c = jnp.sum(jnp.where(lane == NH_A + h, gcol, 0.0), axis=1, keepdims=True)
        f_c = jnp.sum(jnp.where(causal, lf_r, 0.0), axis=1, keepdims=True)
        f_r = jnp.sum(jnp.where(t_idx <= s_idx, lf_c, 0.0), axis=0, keepdims=True)
        f_end = jnp.sum(lf_r, axis=1, keepdims=True)
        m_prev = m_s[h]
        c_prev = c_s[h]
        n_prev = n_s[h]

        d = jnp.where(causal, f_c - f_r + ig_r, -jnp.inf)
        g = f_c + m_prev
        m_t = jnp.maximum(g, jnp.max(d, axis=1, keepdims=True))
        w = jnp.exp(d - m_t)
        inter = jnp.exp(g - m_t)
        s = lax.dot_general(qb, kb, (((1,), (1,)), ((), ())), preferred_element_type=F32) * w
        num = jnp.dot(s.astype(BF16), vb, preferred_element_type=F32)
        num = num + inter * jnp.dot(qb, c_prev.astype(BF16), preferred_element_type=F32)
        den = jnp.sum(s, axis=1, keepdims=True) + inter * jnp.sum(q * n_prev, axis=1, keepdims=True)
        hh = num / jnp.maximum(jnp.abs(den), jnp.exp(-m_t))

        d_end_r = f_end - f_r + ig_r
        d_end_c = f_end - f_c + ig_c
        m_new = jnp.maximum(f_end + m_prev, jnp.max(d_end_r, axis=1, keepdims=True))
        w_end_c = jnp.exp(d_end_c - m_new)
        decay = jnp.exp(f_end + m_prev - m_new)
        kw = k * w_end_c
        c_s[h] = decay * c_prev + lax.dot_general(
            kw.astype(BF16), vb, (((0,), (0,)), ((), ())), preferred_element_type=F32)
        n_s[h] = decay * n_prev + jnp.sum(kw, axis=0, keepdims=True)
        m_s[h] = m_new

        mu = jnp.mean(hh, axis=1, keepdims=True)
        hc = hh - mu
        hn = hc * lax.rsqrt(jnp.mean(hc * hc, axis=1, keepdims=True) + HEAD_EPS)
        hn = hn * nw_ref[:, vs]
        y_ref[:, vs] = (jax.nn.sigmoid(o_ref[:, vs]) * hn).astype(BF16)
        return carry

    lax.fori_loop(0, NH_A, head, 0)

    @pl.when(c == nc - 1)
    def _():
        ct_ref[0] = c_s[...]
        nt_ref[0] = n_s[...]
        mt_ref[0] = m_s[...]


def mlstm_scan(proj, gates, norm_w, *, row0, n_seq, n_chunk, L, init=None):
    rb0 = row0 // L
    rows = n_seq * n_chunk * L
    g_rows = lax.slice_in_dim(gates, row0, row0 + rows, axis=0)
    g_t = g_rows[:, :2 * NH_A].T.reshape(2 * NH_A, n_seq * n_chunk, 1, L)
    rmap = lambda s, c: (rb0 + s * n_chunk + c)
    in_specs = [
        pl.BlockSpec((L, QK_A), lambda s, c: (rmap(s, c), 0)),
        pl.BlockSpec((L, QK_A), lambda s, c: (rmap(s, c), 1)),
        pl.BlockSpec((L, VD_A), lambda s, c: (rmap(s, c), 1)),
        pl.BlockSpec((L, VD_A), lambda s, c: (rmap(s, c), 2)),
        pl.BlockSpec((L, LANES), lambda s, c: (rmap(s, c), 0)),
        pl.BlockSpec((2 * NH_A, 1, 1, L), lambda s, c: (0, s * n_chunk + c, 0, 0)),
        pl.BlockSpec((1, VD_A), lambda s, c: (0, 0)),
    ]
    args = [proj, proj, proj, proj, gates, g_t, norm_w.reshape(1, VD_A)]
    state_specs = [
        pl.BlockSpec((1, NH_A, DK_A, DV_A), lambda s, c: (s, 0, 0, 0)),
        pl.BlockSpec((1, NH_A, 1, DK_A), lambda s, c: (s, 0, 0, 0)),
        pl.BlockSpec((1, NH_A, 1, 1), lambda s, c: (s, 0, 0, 0)),
    ]
    if init is not None:
        c0, n0, m0, layer = init
        nl = c0.shape[0]
        in_specs += [
            pl.BlockSpec((None, 1, NH_A, DK_A, DV_A), lambda s, c: (layer, s, 0, 0, 0)),
            pl.BlockSpec((None, 1, NH_A, 1, DK_A), lambda s, c: (layer, s, 0, 0, 0)),
            pl.BlockSpec((None, 1, NH_A, 1, 1), lambda s, c: (layer, s, 0, 0, 0)),
        ]
        args += [c0, n0.reshape(nl, n_seq, NH_A, 1, DK_A), m0.reshape(nl, n_seq, NH_A, 1, 1)]
    est = 2 * (2 * _nbytes((L, QK_A), F32) + 2 * _nbytes((L, VD_A), F32) + _nbytes((L, VD_A), BF16))
    est += (3 if init is None else 5) * _nbytes((NH_A, DK_A, DV_A), F32)
    est += 8 * _nbytes((L, max(L, DV_A)), F32)
    y, ct, nt, mt = pl.pallas_call(
        functools.partial(_mlstm_kernel, L=L, has_init=init is not None),
        grid=(n_seq, n_chunk),
        in_specs=in_specs,
        out_specs=[pl.BlockSpec((L, VD_A), lambda s, c: (s * n_chunk + c, 0))] + state_specs,
        out_shape=[
            jax.ShapeDtypeStruct((rows, VD_A), BF16),
            jax.ShapeDtypeStruct((n_seq, NH_A, DK_A, DV_A), F32),
            jax.ShapeDtypeStruct((n_seq, NH_A, 1, DK_A), F32),
            jax.ShapeDtypeStruct((n_seq, NH_A, 1, 1), F32),
        ],
        scratch_shapes=[
            pltpu.VMEM((NH_A, DK_A, DV_A), F32),
            pltpu.VMEM((NH_A, 1, DK_A), F32),
            pltpu.VMEM((NH_A, 1, 1), F32),
        ],
        compiler_params=pltpu.CompilerParams(
            dimension_semantics=("arbitrary", "arbitrary"), vmem_limit_bytes=_vmem_limit(est)),
        name="mlstm_scan_init" if init is not None else "mlstm_scan",
    )(*args)
    return y, ct, nt.reshape(n_seq, NH_A, DK_A), mt.reshape(n_seq, NH_A)


def _hgrn_kernel(q_ref, f_ref, i_ref, g_ref, bl_ref, nw_ref, s0_ref, y_ref, st_ref, s_s,
                 *, layer_j, n_prompt_chunks, head_unroll):
    L = CHUNK
    g_id = pl.program_id(0)

    @pl.when(g_id == 0)
    def _():
        s_s[...] = jnp.zeros_like(s_s)

    @pl.when(g_id >= n_prompt_chunks)
    def _():
        def load(h, carry):
            s_s[h] = s0_ref[0, h].T
            return carry
        lax.fori_loop(0, NH_B, load, 0)

    row = lax.broadcasted_iota(jnp.int32, (L, DK_B), 0)
    t_idx = lax.broadcasted_iota(jnp.int32, (L, L), 0)
    s_idx = lax.broadcasted_iota(jnp.int32, (L, L), 1)
    levels = [1, 2, 4, 8, 16, 32]
    upper = {mm: (row & mm) != 0 for mm in levels}
    pair = {mm: ((t_idx ^ s_idx) < 2 * mm) & ((t_idx & mm) != 0) & ((s_idx & mm) == 0) for mm in levels}
    diag = t_idx == s_idx
    nt_dims = (((1,), (1,)), ((), ()))
    tn_dims = (((0,), (0,)), ((), ()))

    def head(h, carry):
        sl = pl.ds(pl.multiple_of(h * DK_B, DK_B), DK_B)
        q = q_ref[:, sl]
        fz = f_ref[:, sl]
        inp = i_ref[:, sl]
        gate = g_ref[:, sl]
        bl = bl_ref[:, sl]
        e = jnp.exp(bl - jnp.max(bl, axis=0, keepdims=True))
        p = e / jnp.sum(e, axis=0, keepdims=True)
        lb = jnp.zeros((1, DK_B), F32)
        for r in range(1, layer_j + 1):
            lb = lb + p[r:r + 1]
        sig = jax.nn.sigmoid(fz)
        oml = 1.0 - lb
        logf = jnp.log(lb + oml * sig)
        k = oml * (1.0 - sig)

        ib = inp.astype(BF16)
        s_acc = jnp.where(
            diag, lax.dot_general(q.astype(BF16), k.astype(BF16), nt_dims, preferred_element_type=F32), 0.0)
        p_sum = logf
        r_sum = jnp.zeros_like(logf)
        g_sum = logf
        for mm in levels:
            up = upper[mm]
            sib = jnp.where(up, pltpu.roll(g_sum, mm, 0), pltpu.roll(g_sum, L - mm, 0))
            x = (jnp.where(up, q, k) * jnp.exp(jnp.where(up, p_sum, r_sum))).astype(BF16)
            s_m = lax.dot_general(x, x, nt_dims, preferred_element_type=F32)
            s_acc = jnp.where(pair[mm], s_m, s_acc)
            p_sum = p_sum + jnp.where(up, sib, 0.0)
            r_sum = r_sum + jnp.where(up, 0.0, sib)
            g_sum = g_sum + sib
        s_prev = s_s[h]
        o = lax.dot_general((q * jnp.exp(p_sum)).astype(BF16), s_prev.astype(BF16), nt_dims,
                            preferred_element_type=F32)
        o = o + jnp.dot(s_acc.astype(BF16), ib, preferred_element_type=F32)
        k_end = (k * jnp.exp(r_sum)).astype(BF16)
        s_s[h] = jnp.exp(g_sum[0:1]) * s_prev + lax.dot_general(ib, k_end, tn_dims, preferred_element_type=F32)

        o = o * lax.rsqrt(jnp.mean(o * o, axis=1, keepdims=True) + HEAD_EPS)
        o = o * nw_ref[:, sl] * (gate * jax.nn.sigmoid(gate))
        y_ref[:, sl] = o.astype(BF16)
        return carry

    lax.fori_loop(0, NH_B, head, 0, unroll=head_unroll)

    @pl.when(g_id >= n_prompt_chunks - 1)
    def _():
        def store(h, carry):
            st_ref[0, h] = s_s[h].T
            return carry
        lax.fori_loop(0, NH_B, store, 0)


def hgrn_scan(proj, b_lower, norm_w, s0, *, layer_j, n_prompt_chunks, head_unroll=8):
    m = proj.shape[0]
    n_chunks = m // CHUNK
    n_seq = n_chunks - n_prompt_chunks + 1
    n_b = b_lower.shape[0]
    col = lambda cb: pl.BlockSpec((CHUNK, D_MODEL), lambda g: (g, cb))
    est = 2 * (4 * _nbytes((CHUNK, D_MODEL), F32) + _nbytes((CHUNK, D_MODEL), BF16))
    est += 5 * _nbytes((NH_B, DK_B, DV_B), F32)
    y, st = pl.pallas_call(
        functools.partial(_hgrn_kernel, layer_j=layer_j, n_prompt_chunks=n_prompt_chunks,
                          head_unroll=head_unroll),
        grid=(n_chunks,),
        in_specs=[col(0), col(1), col(2), col(3),
                  pl.BlockSpec((n_b, D_MODEL), lambda g: (0, 0)),
                  pl.BlockSpec((1, D_MODEL), lambda g: (0, 0)),
                  pl.BlockSpec((None, 1, NH_B, DK_B, DV_B),
                               lambda g: (layer_j, jnp.maximum(g - n_prompt_chunks, 0), 0, 0, 0))],
        out_specs=[pl.BlockSpec((CHUNK, D_MODEL), lambda g: (g, 0)),
                   pl.BlockSpec((1, NH_B, DK_B, DV_B),
                                lambda g: (jnp.maximum(g - (n_prompt_chunks - 1), 0), 0, 0, 0))],
        out_shape=[jax.ShapeDtypeStruct((m, D_MODEL), BF16),
                   jax.ShapeDtypeStruct((n_seq, NH_B, DK_B, DV_B), F32)],
        scratch_shapes=[pltpu.VMEM((NH_B, DV_B, DK_B), F32)],
        compiler_params=pltpu.CompilerParams(
            dimension_semantics=("arbitrary",), vmem_limit_bytes=_vmem_limit(est)),
        name="hgrn_scan",
    )(proj, proj, proj, proj, b_lower, norm_w.reshape(1, D_MODEL), s0)
    return y, st


def _ffn_kernel(x_ref, wg_ref, wu_ref, wd_ref, cw_ref, cb_ref, st_ref, h_ref, so_ref, wdb_ref,
                carry_ref, wgb_ref, wub_ref, *, bm, n_prompt_tiles, seq_len):
    i = pl.program_id(1)

    @pl.when(i == 0)
    def _():
        wgb_ref[...] = wg_ref[...].astype(BF16)
        wub_ref[...] = wu_ref[...].astype(BF16)
        wdb_ref[...] = wd_ref[...].astype(BF16)
        carry_ref[...] = jnp.zeros_like(carry_ref)

    x = x_ref[...]
    a = jnp.dot(x, wgb_ref[...], preferred_element_type=F32)
    u = jnp.dot(x, wub_ref[...], preferred_element_type=F32)
    w0 = cw_ref[0:1, :]
    w1 = cw_ref[1:2, :]
    w2 = cw_ref[2:3, :]
    cb = cb_ref[...]

    def act(a_cur, a_p1, a_p2, u_cur):
        c = cb + a_p2 * w0
        c = c + a_p1 * w1
        c = c + a_cur * w2
        return (c * jax.nn.sigmoid(c) * u_cur).astype(BF16)

    h_ref[...] = act(a, pltpu.roll(a, 1, 0), pltpu.roll(a, 2, 0), u)

    rows8 = lax.broadcasted_iota(jnp.int32, (SUBLANES, a.shape[1]), 0)

    def patch(r0, prev8):
        blk = a[r0:r0 + SUBLANES]
        p1 = jnp.where(rows8 < 1, pltpu.roll(prev8, 1, 0), pltpu.roll(blk, 1, 0))
        p2 = jnp.where(rows8 < 2, pltpu.roll(prev8, 2, 0), pltpu.roll(blk, 2, 0))
        h_ref[r0:r0 + SUBLANES, :] = act(blk, p1, p2, u[r0:r0 + SUBLANES])

    is_sample = i >= n_prompt_tiles
    patch(0, jnp.where(is_sample, st_ref[0], carry_ref[...]))
    n_groups = bm // seq_len

    @pl.when(is_sample)
    def _():
        for s in range(1, n_groups):
            patch(s * seq_len, st_ref[s])
        for s in range(n_groups):
            so_ref[1 + s] = a[(s + 1) * seq_len - SUBLANES:(s + 1) * seq_len]

    carry_ref[...] = a[bm - SUBLANES:bm]

    @pl.when(i == n_prompt_tiles - 1)
    def _():
        so_ref[0] = a[bm - SUBLANES:bm]


def ffn_gate_up(xb, wg, wu, wd, conv_w, conv_b, state8, layer, *, bm, bn, n_prompt_tiles, seq_len):
    m, d = xb.shape
    n = wg.shape[2]
    assert m % bm == 0 and n % bn == 0
    n_seq_out = state8.shape[0] + 1
    est = 2 * (_nbytes((bm, d), BF16) + 3 * _nbytes((d, bn), F32) + _nbytes((bm, bn), BF16))
    est += 4 * _nbytes((d, bn), BF16)
    est += 4 * _nbytes((bm, bn), F32) + 4 * _nbytes((n_seq_out, SUBLANES, bn), F32)
    col = lambda j, i: (layer, 0, j)
    return pl.pallas_call(
        functools.partial(_ffn_kernel, bm=bm, n_prompt_tiles=n_prompt_tiles, seq_len=seq_len),
        grid=(n // bn, m // bm),
        in_specs=[pl.BlockSpec((bm, d), lambda j, i: (i, 0)),
                  pl.BlockSpec((None, d, bn), col),
                  pl.BlockSpec((None, d, bn), col),
                  pl.BlockSpec((None, bn, d), lambda j, i: (layer, j, 0)),
                  pl.BlockSpec((None, CONV_W, bn), col),
                  pl.BlockSpec((None, 1, bn), col),
                  pl.BlockSpec((state8.shape[0], SUBLANES, bn), lambda j, i: (0, 0, j))],
        out_specs=[pl.BlockSpec((bm, bn), lambda j, i: (i, j)),
                   pl.BlockSpec((n_seq_out, SUBLANES, bn), lambda j, i: (0, 0, j)),
                   pl.BlockSpec((bn, d), lambda j, i: (j, 0))],
        out_shape=[jax.ShapeDtypeStruct((m, n), BF16),
                   jax.ShapeDtypeStruct((n_seq_out, SUBLANES, n), F32),
                   jax.ShapeDtypeStruct((n, d), BF16)],
        scratch_shapes=[pltpu.VMEM((SUBLANES, bn), F32),
                        pltpu.VMEM((d, bn), BF16),
                        pltpu.VMEM((d, bn), BF16)],
        compiler_params=pltpu.CompilerParams(
            dimension_semantics=("arbitrary", "arbitrary"), vmem_limit_bytes=_vmem_limit(est)),
        name="ffn_gate_up",
    )(xb, wg, wu, wd, conv_w, conv_b, state8)


def kernel(x_prompt, x_sample, state_mlstm_C, state_mlstm_n, state_mlstm_m, state_hgrn_S, state_ffn_conv,
           ln_g, ln_b, w_a_in, b_a_gate, a_norm_w, w_a_out, w_b_in, b_lower, b_norm_w, w_b_out,
           w_ffn_gate, w_ffn_up, w_ffn_down, ffn_conv_w, ffn_conv_b):
    bp, tp, d = x_prompt.shape
    bs, ts, _ = x_sample.shape
    mp = bp * tp
    ms = bs * ts
    m = mp + ms
    assert bp == 1 and ts == CHUNK and d == D_MODEL
    ffn_bm = ms
    assert mp % ffn_bm == 0
    l_prompt = 128

    x = jnp.concatenate([x_prompt.reshape(mp, d), x_sample.reshape(ms, d)], axis=0)
    xb = x.astype(BF16)

    p_c, p_n, p_m, p_s, p_conv = [], [], [], [], []
    s_c, s_n, s_m, s_s, s_conv = [], [], [], [], []
    for l in range(DEPTH):
        j = l // 2
        if l % 2 == 0:
            w_gate = jnp.pad(w_a_in[j, :, A_MAIN:], ((0, 0), (0, LANES - 2 * NH_A))).astype(BF16)
            b_gate = jnp.pad(b_a_gate[j], (0, LANES - 2 * NH_A)).reshape(1, LANES)
            proj = matmul_wcast(xb, w_a_in, j, n_cols=A_MAIN, bm=544, bn=1024, name="mlstm_in_proj")
            gates = mlstm_gates(xb, w_gate, b_gate)
            y_p, c_p, n_p, m_p = mlstm_scan(proj, gates, a_norm_w[j], row0=0, n_seq=1,
                                            n_chunk=mp // l_prompt, L=l_prompt)
            y_s, c_s2, n_s2, m_s2 = mlstm_scan(
                proj, gates, a_norm_w[j], row0=mp, n_seq=bs, n_chunk=1, L=ts,
                init=(state_mlstm_C, state_mlstm_n, state_mlstm_m, j))
            p_c.append(c_p); p_n.append(n_p); p_m.append(m_p)
            s_c.append(c_s2); s_n.append(n_s2); s_m.append(m_s2)
            y_mix = jnp.concatenate([y_p, y_s], axis=0)
            w_out = w_a_out
        else:
            proj = matmul_wcast(xb, w_b_in, j, bm=544, bn=1024, name="hgrn_in_proj")
            y_mix, st = hgrn_scan(proj, b_lower, b_norm_w[j], state_hgrn_S, layer_j=j,
                                  n_prompt_chunks=mp // CHUNK)
            p_s.append(st[:1]); s_s.append(st[1:])
            w_out = w_b_out
        y = matmul_wcast(y_mix, w_out, j, bm=544, bn=1024, res=x, alpha=ALPHA, name="mixer_out_proj")
        x, xb = layer_norm(y, ln_g[l, 0], ln_b[l, 0])

        st8 = jnp.pad(state_ffn_conv[l], ((0, 0), (SUBLANES - (CONV_W - 1), 0), (0, 0)))
        hmid, so, wd_b = ffn_gate_up(xb, w_ffn_gate, w_ffn_up, w_ffn_down, ffn_conv_w,
                                     ffn_conv_b.reshape(DEPTH, 1, D_FF), st8, l, bm=ffn_bm, bn=256,
                                     n_prompt_tiles=mp // ffn_bm, seq_len=ts)
        conv_new = so[:, SUBLANES - (CONV_W - 1):, :]
        p_conv.append(conv_new[:1]); s_conv.append(conv_new[1:])
        y = matmul_xres(hmid, wd_b, x, bm=544, bn=256, alpha=ALPHA, name="ffn_down_proj")
        x, xb = layer_norm(y, ln_g[l, 1], ln_b[l, 1])

    y_prompt = x[:mp].reshape(bp, tp, d)
    y_sample = x[mp:].reshape(bs, ts, d)
    return (y_prompt, y_sample,
            jnp.stack(p_c), jnp.stack(p_n), jnp.stack(p_m), jnp.stack(p_s), jnp.stack(p_conv),
            jnp.stack(s_c), jnp.stack(s_n), jnp.stack(s_m), jnp.stack(s_s), jnp.stack(s_conv))
```

```python
import functools

import jax
import jax.numpy as jnp
from jax import lax
from jax.experimental import pallas as pl
from jax.experimental.pallas import tpu as pltpu

F32 = jnp.float32
BF16 = jnp.bfloat16

D_MODEL = 4096
DEPTH = 4
CHUNK = 64
NH_A = 8
DK_A = 256
DV_A = 512
QK_A = NH_A * DK_A
VD_A = NH_A * DV_A
A_MAIN = 2 * QK_A + 2 * VD_A
GATE_SOFTCAP = 15.0
NH_B = 32
DK_B = 128
DV_B = 128
D_FF = 11008
CONV_W = 3
ALPHA = (2 * DEPTH) ** 0.25
LN_EPS = 1e-5
HEAD_EPS = 1e-6

LANES = 128
SUBLANES = 8
VMEM_LIMIT_CAP = 58 * 1024 * 1024


def _vmem_limit(nbytes):
    return int(min(VMEM_LIMIT_CAP, nbytes * 1.2 + (6 << 20)))


def _nbytes(shape, dtype):
    n = 1
    for s in shape:
        n *= s
    return n * jnp.dtype(dtype).itemsize


def _finish(acc, r_ref, o_ref, alpha):
    if r_ref is not None:
        acc = alpha * r_ref[...] + acc
    o_ref[...] = acc.astype(o_ref.dtype)


def _mm_wcast_kernel(*refs, nn, nm, w_t, has_res, alpha):
    if has_res:
        x_ref, wc_ref, r_ref, o_ref, wb_ref = refs
    else:
        x_ref, wc_ref, o_ref, wb_ref = refs
        r_ref = None
    p = pl.program_id(0)
    i = pl.program_id(1)
    crows = wc_ref.shape[0]

    @pl.when(p < nn)
    def _():
        wb_ref[p % 2, pl.ds(pl.multiple_of(i * crows, crows), crows), :] = wc_ref[...].astype(BF16)

    @pl.when(p > 0)
    def _():
        wb = wb_ref[(p + 1) % 2]
        if w_t:
            acc = lax.dot_general(x_ref[...], wb, (((1,), (1,)), ((), ())), preferred_element_type=F32)
        else:
            acc = jnp.dot(x_ref[...], wb, preferred_element_type=F32)
        _finish(acc, r_ref, o_ref, alpha)


def matmul_wcast(x, w, layer, *, bm, bn, n_cols=None, w_t=False, res=None, alpha=1.0, out_dtype=F32,
                 name="mm"):
    m, kdim = x.shape
    n = w.shape[1 if w_t else 2] if n_cols is None else n_cols
    nm, nn = m // bm, n // bn
    crows = (bn if w_t else kdim) // nm
    assert m % bm == 0 and n % bn == 0 and (bn if w_t else kdim) % nm == 0 and crows % SUBLANES == 0
    row = lambda p, i: jnp.where(p > 0, i, 0)
    xi = lambda p, i: (row(p, i), 0)
    oi = lambda p, i: (row(p, i), jnp.maximum(p - 1, 0))
    chunk = lambda p, i: jnp.where(p < nn, i, nm - 1)
    tile = lambda p: jnp.minimum(p, nn - 1)
    if w_t:
        wspec = pl.BlockSpec((None, crows, kdim), lambda p, i: (layer, tile(p) * nm + chunk(p, i), 0))
        wb_shape = (2, bn, kdim)
    else:
        wspec = pl.BlockSpec((None, crows, bn), lambda p, i: (layer, chunk(p, i), tile(p)))
        wb_shape = (2, kdim, bn)
    in_specs = [pl.BlockSpec((bm, kdim), xi), wspec]
    args = [x, w]
    est = 2 * (_nbytes((bm, kdim), BF16) + _nbytes(wspec.block_shape[1:], F32) + _nbytes((bm, bn), out_dtype))
    est += _nbytes((bm, bn), F32) + _nbytes(wb_shape, BF16)
    if res is not None:
        in_specs.append(pl.BlockSpec((bm, bn), oi))
        args.append(res)
        est += 2 * _nbytes((bm, bn), res.dtype)
    return pl.pallas_call(
        functools.partial(_mm_wcast_kernel, nn=nn, nm=nm, w_t=w_t, has_res=res is not None, alpha=alpha),
        grid=(nn + 1, nm),
        in_specs=in_specs,
        out_specs=pl.BlockSpec((bm, bn), oi),
        out_shape=jax.ShapeDtypeStruct((m, n), out_dtype),
        scratch_shapes=[pltpu.VMEM(wb_shape, BF16)],
        compiler_params=pltpu.CompilerParams(
            dimension_semantics=("arbitrary", "arbitrary"), vmem_limit_bytes=_vmem_limit(est)),
        name=name,
    )(*args)


def _mm_kernel(x_ref, w_ref, r_ref, o_ref, *, alpha):
    _finish(jnp.dot(x_ref[...], w_ref[...], preferred_element_type=F32), r_ref, o_ref, alpha)


def matmul_xres(x, w, res, *, bm, bn, alpha, name="mm"):
    m, kdim = x.shape
    n = w.shape[1]
    assert m % bm == 0 and n % bn == 0
    oi = lambda i, j: (i, j)
    est = 2 * (_nbytes((bm, kdim), BF16) + _nbytes((kdim, bn), BF16) + 2 * _nbytes((bm, bn), F32))
    est += _nbytes((bm, bn), F32)
    return pl.pallas_call(
        functools.partial(_mm_kernel, alpha=alpha),
        grid=(m // bm, n // bn),
        in_specs=[pl.BlockSpec((bm, kdim), lambda i, j: (i, 0)),
                  pl.BlockSpec((kdim, bn), lambda i, j: (0, j)),
                  pl.BlockSpec((bm, bn), oi)],
        out_specs=pl.BlockSpec((bm, bn), oi),
        out_shape=jax.ShapeDtypeStruct((m, n), F32),
        compiler_params=pltpu.CompilerParams(
            dimension_semantics=("arbitrary", "arbitrary"), vmem_limit_bytes=_vmem_limit(est)),
        name=name,
    )(x, w, res)


def _ln_kernel(y_ref, g_ref, b_ref, o_ref, ob_ref):
    y = y_ref[...]
    mu = jnp.mean(y, axis=-1, keepdims=True)
    yc = y - mu
    var = jnp.mean(yc * yc, axis=-1, keepdims=True)
    out = yc * lax.rsqrt(var + LN_EPS) * g_ref[...] + b_ref[...]
    o_ref[...] = out
    ob_ref[...] = out.astype(BF16)


def layer_norm(y, g, b, *, bm=256):
    m, d = y.shape
    row = pl.BlockSpec((bm, d), lambda i: (i, 0))
    vec = pl.BlockSpec((1, d), lambda i: (0, 0))
    est = 2 * (2 * _nbytes((bm, d), F32) + _nbytes((bm, d), BF16)) + 2 * _nbytes((bm, d), F32)
    return pl.pallas_call(
        _ln_kernel,
        grid=(m // bm,),
        in_specs=[row, vec, vec],
        out_specs=[row, row],
        out_shape=[jax.ShapeDtypeStruct((m, d), F32), jax.ShapeDtypeStruct((m, d), BF16)],
        compiler_params=pltpu.CompilerParams(
            dimension_semantics=("arbitrary",), vmem_limit_bytes=_vmem_limit(est)),
        name="layer_norm",
    )(y, g.reshape(1, d), b.reshape(1, d))


def _log_sigmoid(z):
    return jnp.minimum(z, 0.0) - jnp.log1p(jnp.exp(-jnp.abs(z)))


def _gates_kernel(x_ref, w_ref, b_ref, o_ref):
    g = jnp.dot(x_ref[...], w_ref[...], preferred_element_type=F32) + b_ref[...]
    sc = GATE_SOFTCAP * jnp.tanh(g / GATE_SOFTCAP)
    lane = lax.broadcasted_iota(jnp.int32, sc.shape, 1)
    o_ref[...] = jnp.where(lane < NH_A, sc, _log_sigmoid(sc))


def mlstm_gates(xb, w_gate, b_gate, *, bm=1088):
    m, d = xb.shape
    est = 2 * (_nbytes((bm, d), BF16) + _nbytes((d, LANES), BF16) + _nbytes((bm, LANES), F32))
    return pl.pallas_call(
        _gates_kernel,
        grid=(m // bm,),
        in_specs=[pl.BlockSpec((bm, d), lambda i: (i, 0)),
                  pl.BlockSpec((d, LANES), lambda i: (0, 0)),
                  pl.BlockSpec((1, LANES), lambda i: (0, 0))],
        out_specs=pl.BlockSpec((bm, LANES), lambda i: (i, 0)),
        out_shape=jax.ShapeDtypeStruct((m, LANES), F32),
        compiler_params=pltpu.CompilerParams(
            dimension_semantics=("arbitrary",), vmem_limit_bytes=_vmem_limit(est)),
        name="mlstm_gates",
    )(xb, w_gate, b_gate)


def _mlstm_kernel(*refs, L, has_init):
    if has_init:
        (q_ref, k_ref, v_ref, o_ref, gc_ref, gr_ref, nw_ref, c0_ref, n0_ref, m0_ref,
         y_ref, ct_ref, nt_ref, mt_ref, c_s, n_s, m_s) = refs
    else:
        (q_ref, k_ref, v_ref, o_ref, gc_ref, gr_ref, nw_ref,
         y_ref, ct_ref, nt_ref, mt_ref, c_s, n_s, m_s) = refs
    c = pl.program_id(1)
    nc = pl.num_programs(1)

    @pl.when(c == 0)
    def _():
        if has_init:
            c_s[...] = c0_ref[0]
            n_s[...] = n0_ref[0]
            m_s[...] = m0_ref[0]
        else:
            c_s[...] = jnp.zeros_like(c_s)
            n_s[...] = jnp.zeros_like(n_s)
            m_s[...] = jnp.zeros_like(m_s)

    t_idx = lax.broadcasted_iota(jnp.int32, (L, L), 0)
    s_idx = lax.broadcasted_iota(jnp.int32, (L, L), 1)
    causal = s_idx <= t_idx
    lane = lax.broadcasted_iota(jnp.int32, (L, LANES), 1)
    gcol = gc_ref[...]

    def head(h, carry):
        qs = pl.ds(pl.multiple_of(h * DK_A, DK_A), DK_A)
        vs = pl.ds(pl.multiple_of(h * DV_A, DV_A), DV_A)
        q = q_ref[:, qs]
        k = k_ref[:, qs] * (DK_A ** -0.5)
        v = v_ref[:, vs]
        qb = q.astype(BF16)
        kb = k.astype(BF16)
        vb = v.astype(BF16)
        ig_r = gr_ref[h, 0]
        lf_r = gr_ref[NH_A + h, 0]
        ig_c = jnp.sum(jnp.where(lane == h, gcol, 0.0), axis=1, keepdims=True)
        lf_c = jnp.sum(jnp.where(lane == NH_A + h, gcol, 0.0), axis=1, keepdims=True)
        f_c = jnp.sum(jnp.where(causal, lf_r, 0.0), axis=1, keepdims=True)
        f_r = jnp.sum(jnp.where(t_idx <= s_idx, lf_c, 0.0), axis=0, keepdims=True)
        f_end = jnp.sum(lf_r, axis=1, keepdims=True)
        m_prev = m_s[h]
        c_prev = c_s[h]
        n_prev = n_s[h]

        d = jnp.where(causal, f_c - f_r + ig_r, -jnp.inf)
        g = f_c + m_prev
        m_t = jnp.maximum(g, jnp.max(d, axis=1, keepdims=True))
        w = jnp.exp(d - m_t)
        inter = jnp.exp(g - m_t)
        s = lax.dot_general(qb, kb, (((1,), (1,)), ((), ())), preferred_element_type=F32) * w
        num = jnp.dot(s.astype(BF16), vb, preferred_element_type=F32)
        num = num + inter * jnp.dot(qb, c_prev.astype(BF16), preferred_element_type=F32)
        den = jnp.sum(s, axis=1, keepdims=True) + inter * jnp.sum(q * n_prev, axis=1, keepdims=True)
        hh = num / jnp.maximum(jnp.abs(den), jnp.exp(-m_t))

        d_end_r = f_end - f_r + ig_r
        d_end_c = f_end - f_c + ig_c
        m_new = jnp.maximum(f_end + m_prev, jnp.max(d_end_r, axis=1, keepdims=True))
        w_end_c = jnp.exp(d_end_c - m_new)
        decay = jnp.exp(f_end + m_prev - m_new)
        kw = k * w_end_c
        c_s[h] = decay * c_prev + lax.dot_general(
            kw.astype(BF16), vb, (((0,), (0,)), ((), ())), preferred_element_type=F32)
        n_s[h] = decay * n_prev + jnp.sum(kw, axis=0, keepdims=True)
        m_s[h] = m_new

        mu = jnp.mean(hh, axis=1, keepdims=True)
        hc = hh - mu
        hn = hc * lax.rsqrt(jnp.mean(hc * hc, axis=1, keepdims=True) + HEAD_EPS)
        hn = hn * nw_ref[:, vs]
        y_ref[:, vs] = (jax.nn.sigmoid(o_ref[:, vs]) * hn).astype(BF16)
        return carry

    lax.fori_loop(0, NH_A, head, 0, unroll=2)

    @pl.when(c == nc - 1)
    def _():
        ct_ref[0] = c_s[...]
        nt_ref[0] = n_s[...]
        mt_ref[0] = m_s[...]


def mlstm_scan(proj, gates, norm_w, *, row0, n_seq, n_chunk, L, init=None):
    rb0 = row0 // L
    rows = n_seq * n_chunk * L
    g_rows = lax.slice_in_dim(gates, row0, row0 + rows, axis=0)
    g_t = g_rows[:, :2 * NH_A].T.reshape(2 * NH_A, n_seq * n_chunk, 1, L)
    rmap = lambda s, c: (rb0 + s * n_chunk + c)
    in_specs = [
        pl.BlockSpec((L, QK_A), lambda s, c: (rmap(s, c), 0)),
        pl.BlockSpec((L, QK_A), lambda s, c: (rmap(s, c), 1)),
        pl.BlockSpec((L, VD_A), lambda s, c: (rmap(s, c), 1)),
        pl.BlockSpec((L, VD_A), lambda s, c: (rmap(s, c), 2)),
        pl.BlockSpec((L, LANES), lambda s, c: (rmap(s, c), 0)),
        pl.BlockSpec((2 * NH_A, 1, 1, L), lambda s, c: (0, s * n_chunk + c, 0, 0)),
        pl.BlockSpec((1, VD_A), lambda s, c: (0, 0)),
    ]
    args = [proj, proj, proj, proj, gates, g_t, norm_w.reshape(1, VD_A)]
    state_specs = [
        pl.BlockSpec((1, NH_A, DK_A, DV_A), lambda s, c: (s, 0, 0, 0)),
        pl.BlockSpec((1, NH_A, 1, DK_A), lambda s, c: (s, 0, 0, 0)),
        pl.BlockSpec((1, NH_A, 1, 1), lambda s, c: (s, 0, 0, 0)),
    ]
    if init is not None:
        c0, n0, m0, layer = init
        nl = c0.shape[0]
        in_specs += [
            pl.BlockSpec((None, 1, NH_A, DK_A, DV_A), lambda s, c: (layer, s, 0, 0, 0)),
            pl.BlockSpec((None, 1, NH_A, 1, DK_A), lambda s, c: (layer, s, 0, 0, 0)),
            pl.BlockSpec((None, 1, NH_A, 1, 1), lambda s, c: (layer, s, 0, 0, 0)),
        ]
        args += [c0, n0.reshape(nl, n_seq, NH_A, 1, DK_A), m0.reshape(nl, n_seq, NH_A, 1, 1)]
    est = 2 * (2 * _nbytes((L, QK_A), F32) + 2 * _nbytes((L, VD_A), F32) + _nbytes((L, VD_A), BF16))
    est += (3 if init is None else 5) * _nbytes((NH_A, DK_A, DV_A), F32)
    est += 8 * _nbytes((L, max(L, DV_A)), F32)
    y, ct, nt, mt = pl.pallas_call(
        functools.partial(_mlstm_kernel, L=L, has_init=init is not None),
        grid=(n_seq, n_chunk),
        in_specs=in_specs,
        out_specs=[pl.BlockSpec((L, VD_A), lambda s, c: (s * n_chunk + c, 0))] + state_specs,
        out_shape=[
            jax.ShapeDtypeStruct((rows, VD_A), BF16),
            jax.ShapeDtypeStruct((n_seq, NH_A, DK_A, DV_A), F32),
            jax.ShapeDtypeStruct((n_seq, NH_A, 1, DK_A), F32),
            jax.ShapeDtypeStruct((n_seq, NH_A, 1, 1), F32),
        ],
        scratch_shapes=[
            pltpu.VMEM((NH_A, DK_A, DV_A), F32),
            pltpu.VMEM((NH_A, 1, DK_A), F32),
            pltpu.VMEM((NH_A, 1, 1), F32),
        ],
        compiler_params=pltpu.CompilerParams(
            dimension_semantics=("arbitrary", "arbitrary"), vmem_limit_bytes=_vmem_limit(est)),
        name="mlstm_scan_init" if init is not None else "mlstm_scan",
    )(*args)
    return y, ct, nt.reshape(n_seq, NH_A, DK_A), mt.reshape(n_seq, NH_A)


def _hgrn_kernel(q_ref, f_ref, i_ref, g_ref, bl_ref, nw_ref, s0_ref, y_ref, st_ref, s_s,
                 *, layer_j, n_prompt_chunks, head_unroll):
    L = CHUNK
    g_id = pl.program_id(0)

    @pl.when(g_id == 0)
    def _():
        s_s[...] = jnp.zeros_like(s_s)

    @pl.when(g_id >= n_prompt_chunks)
    def _():
        def load(h, carry):
            s_s[h] = s0_ref[0, h].T
            return carry
        lax.fori_loop(0, NH_B, load, 0)

    row = lax.broadcasted_iota(jnp.int32, (L, DK_B), 0)
    t_idx = lax.broadcasted_iota(jnp.int32, (L, L), 0)
    s_idx = lax.broadcasted_iota(jnp.int32, (L, L), 1)
    levels = [1, 2, 4, 8, 16, 32]
    upper = {mm: (row & mm) != 0 for mm in levels}
    pair = {mm: ((t_idx ^ s_idx) < 2 * mm) & ((t_idx & mm) != 0) & ((s_idx & mm) == 0) for mm in levels}
    diag = t_idx == s_idx
    nt_dims = (((1,), (1,)), ((), ()))
    tn_dims = (((0,), (0,)), ((), ()))

    def head(h, carry):
        sl = pl.ds(pl.multiple_of(h * DK_B, DK_B), DK_B)
        q = q_ref[:, sl]
        fz = f_ref[:, sl]
        inp = i_ref[:, sl]
        gate = g_ref[:, sl]
        bl = bl_ref[:, sl]
        e = jnp.exp(bl - jnp.max(bl, axis=0, keepdims=True))
        p = e / jnp.sum(e, axis=0, keepdims=True)
        lb = jnp.zeros((1, DK_B), F32)
        for r in range(1, layer_j + 1):
            lb = lb + p[r:r + 1]
        sig = jax.nn.sigmoid(fz)
        oml = 1.0 - lb
        logf = jnp.log(lb + oml * sig)
        k = oml * (1.0 - sig)

        ib = inp.astype(BF16)
        s_acc = jnp.where(
            diag, lax.dot_general(q.astype(BF16), k.astype(BF16), nt_dims, preferred_element_type=F32), 0.0)
        p_sum = logf
        r_sum = jnp.zeros_like(logf)
        g_sum = logf
        for mm in levels:
            up = upper[mm]
            sib = jnp.where(up, pltpu.roll(g_sum, mm, 0), pltpu.roll(g_sum, L - mm, 0))
            x = (jnp.where(up, q, k) * jnp.exp(jnp.where(up, p_sum, r_sum))).astype(BF16)
            s_m = lax.dot_general(x, x, nt_dims, preferred_element_type=F32)
            s_acc = jnp.where(pair[mm], s_m, s_acc)
            p_sum = p_sum + jnp.where(up, sib, 0.0)
            r_sum = r_sum + jnp.where(up, 0.0, sib)
            g_sum = g_sum + sib
        s_prev = s_s[h]
        o = lax.dot_general((q * jnp.exp(p_sum)).astype(BF16), s_prev.astype(BF16), nt_dims,
                            preferred_element_type=F32)
        o = o + jnp.dot(s_acc.astype(BF16), ib, preferred_element_type=F32)
        k_end = (k * jnp.exp(r_sum)).astype(BF16)
        s_s[h] = jnp.exp(g_sum[0:1]) * s_prev + lax.dot_general(ib, k_end, tn_dims, preferred_element_type=F32)

        o = o * lax.rsqrt(jnp.mean(o * o, axis=1, keepdims=True) + HEAD_EPS)
        o = o * nw_ref[:, sl] * (gate * jax.nn.sigmoid(gate))
        y_ref[:, sl] = o.astype(BF16)
        return carry

    lax.fori_loop(0, NH_B, head, 0, unroll=head_unroll)

    @pl.when(g_id >= n_prompt_chunks - 1)
    def _():
        def store(h, carry):
            st_ref[0, h] = s_s[h].T
            return carry
        lax.fori_loop(0, NH_B, store, 0)


def hgrn_scan(proj, b_lower, norm_w, s0, *, layer_j, n_prompt_chunks, head_unroll=8):
    m = proj.shape[0]
    n_chunks = m // CHUNK
    n_seq = n_chunks - n_prompt_chunks + 1
    n_b = b_lower.shape[0]
    col = lambda cb: pl.BlockSpec((CHUNK, D_MODEL), lambda g: (g, cb))
    est = 2 * (4 * _nbytes((CHUNK, D_MODEL), F32) + _nbytes((CHUNK, D_MODEL), BF16))
    est += 5 * _nbytes((NH_B, DK_B, DV_B), F32)
    y, st = pl.pallas_call(
        functools.partial(_hgrn_kernel, layer_j=layer_j, n_prompt_chunks=n_prompt_chunks,
                          head_unroll=head_unroll),
        grid=(n_chunks,),
        in_specs=[col(0), col(1), col(2), col(3),
                  pl.BlockSpec((n_b, D_MODEL), lambda g: (0, 0)),
                  pl.BlockSpec((1, D_MODEL), lambda g: (0, 0)),
                  pl.BlockSpec((None, 1, NH_B, DK_B, DV_B),
                               lambda g: (layer_j, jnp.maximum(g - n_prompt_chunks, 0), 0, 0, 0))],
        out_specs=[pl.BlockSpec((CHUNK, D_MODEL), lambda g: (g, 0)),
                   pl.BlockSpec((1, NH_B, DK_B, DV_B),
                                lambda g: (jnp.maximum(g - (n_prompt_chunks - 1), 0), 0, 0, 0))],
        out_shape=[jax.ShapeDtypeStruct((m, D_MODEL), BF16),
                   jax.ShapeDtypeStruct((n_seq, NH_B, DK_B, DV_B), F32)],
        scratch_shapes=[pltpu.VMEM((NH_B, DV_B, DK_B), F32)],
        compiler_params=pltpu.CompilerParams(
            dimension_semantics=("arbitrary",), vmem_limit_bytes=_vmem_limit(est)),
        name="hgrn_scan",
    )(proj, proj, proj, proj, b_lower, norm_w.reshape(1, D_MODEL), s0)
    return y, st


def _ffn_kernel(x_ref, wgc_ref, wuc_ref, wdc_ref, cw_ref, cb_ref, st_ref, h_ref, so_ref, wdb_ref,
                carry_ref, wgb_ref, wub_ref, *, nm, rb, prompt_end, sample_starts, seq_len):
    p = pl.program_id(0)
    i = pl.program_id(1)
    bm = x_ref.shape[0]
    bn = h_ref.shape[1]
    crows = wgc_ref.shape[0]

    def cast_next_tile():
        rows = pl.ds(pl.multiple_of(i * crows, crows), crows)
        wgb_ref[p % 2, rows, :] = wgc_ref[...].astype(BF16)
        wub_ref[p % 2, rows, :] = wuc_ref[...].astype(BF16)

    @pl.when(p == 0)
    def _():
        cast_next_tile()

    @pl.when(p > 0)
    def _():
        slot = (p + 1) % 2
        w0 = cw_ref[0:1, :]
        w1 = cw_ref[1:2, :]
        w2 = cw_ref[2:3, :]
        cb = cb_ref[...]
        rows8 = lax.broadcasted_iota(jnp.int32, (SUBLANES, bn), 0)
        is_last = i == nm - 1

        def act(a_cur, a_p1, a_p2, u_cur):
            c = cb + a_p2 * w0
            c = c + a_p1 * w1
            c = c + a_cur * w2
            return (c * jax.nn.sigmoid(c) * u_cur).astype(BF16)

        prev8 = jnp.where(i == 0, 0.0, carry_ref[...])
        for r0 in range(0, bm, rb):
            xs = x_ref[r0:r0 + rb, :]
            a = jnp.dot(xs, wgb_ref[slot], preferred_element_type=F32)
            u = jnp.dot(xs, wub_ref[slot], preferred_element_type=F32)
            h_ref[r0:r0 + rb, :] = act(a, pltpu.roll(a, 1, 0), pltpu.roll(a, 2, 0), u)

            def patch(off, before8):
                blk = a[off:off + SUBLANES]
                p1 = jnp.where(rows8 < 1, pltpu.roll(before8, 1, 0), pltpu.roll(blk, 1, 0))
                p2 = jnp.where(rows8 < 2, pltpu.roll(before8, 2, 0), pltpu.roll(blk, 2, 0))
                h_ref[r0 + off:r0 + off + SUBLANES, :] = act(blk, p1, p2, u[off:off + SUBLANES])

            patch(0, prev8)
            for s, start in enumerate(sample_starts):
                off = start - r0
                if SUBLANES <= off < rb:
                    patch(off, jnp.where(is_last, st_ref[s], a[off - SUBLANES:off]))
                end = off + seq_len
                if SUBLANES <= end <= rb:
                    so_ref[1 + s] = a[end - SUBLANES:end]
            if SUBLANES <= prompt_end - r0 <= rb:
                so_ref[0] = a[prompt_end - r0 - SUBLANES:prompt_end - r0]
            prev8 = a[rb - SUBLANES:rb]
        carry_ref[...] = prev8

        cast_next_tile()
        wdb_ref[...] = wdc_ref[...].astype(BF16)


def ffn_gate_up(xb, wg, wu, wd, conv_w, conv_b, state8, layer, *, bm, bn, rb, n_prompt_rows, seq_len):
    m, d = xb.shape
    n = wg.shape[2]
    nm, nn = m // bm, n // bn
    crows = d // nm
    drows = bn // nm
    n_seq = state8.shape[0]
    tile0 = (nm - 1) * bm
    prompt_end = n_prompt_rows - tile0
    sample_starts = tuple(prompt_end + s * seq_len for s in range(n_seq))
    assert m % bm == 0 and n % bn == 0 and bm % rb == 0 and rb % SUBLANES == 0
    assert d % nm == 0 and crows % SUBLANES == 0 and bn % nm == 0 and drows % SUBLANES == 0
    assert m == n_prompt_rows + n_seq * seq_len and SUBLANES <= prompt_end
    assert all(st % SUBLANES == 0 and st % rb != 0 for st in sample_starts)
    est = 2 * (_nbytes((bm, d), BF16) + 2 * _nbytes((crows, bn), F32) + _nbytes((bm, bn), BF16))
    est += 2 * (_nbytes((drows, d), F32) + _nbytes((drows, d), BF16))
    est += 4 * _nbytes((d, bn), BF16)
    est += 6 * _nbytes((rb, bn), F32) + 4 * _nbytes((n_seq + 1, SUBLANES, bn), F32)
    row = lambda p, i: jnp.where(p > 0, i, 0)
    jcol = lambda p: jnp.maximum(p - 1, 0)
    wchunk = lambda p, i: (layer, jnp.where(p < nn, i, nm - 1), jnp.minimum(p, nn - 1))
    dchunk = lambda p, i: jcol(p) * nm + row(p, i)
    return pl.pallas_call(
        functools.partial(_ffn_kernel, nm=nm, rb=rb, prompt_end=prompt_end, sample_starts=sample_starts,
                          seq_len=seq_len),
        grid=(nn + 1, nm),
        in_specs=[pl.BlockSpec((bm, d), lambda p, i: (row(p, i), 0)),
                  pl.BlockSpec((None, crows, bn), wchunk),
                  pl.BlockSpec((None, crows, bn), wchunk),
                  pl.BlockSpec((None, drows, d), lambda p, i: (layer, dchunk(p, i), 0)),
                  pl.BlockSpec((None, CONV_W, bn), lambda p, i: (layer, 0, jcol(p))),
                  pl.BlockSpec((None, 1, bn), lambda p, i: (layer, 0, jcol(p))),
                  pl.BlockSpec((n_seq, SUBLANES, bn), lambda p, i: (0, 0, jcol(p)))],
        out_specs=[pl.BlockSpec((bm, bn), lambda p, i: (row(p, i), jcol(p))),
                   pl.BlockSpec((n_seq + 1, SUBLANES, bn), lambda p, i: (0, 0, jcol(p))),
                   pl.BlockSpec((drows, d), lambda p, i: (dchunk(p, i), 0))],
        out_shape=[jax.ShapeDtypeStruct((m, n), BF16),
                   jax.ShapeDtypeStruct((n_seq + 1, SUBLANES, n), F32),
                   jax.ShapeDtypeStruct((n, d), BF16)],
        scratch_shapes=[pltpu.VMEM((SUBLANES, bn), F32),
                        pltpu.VMEM((2, d, bn), BF16),
                        pltpu.VMEM((2, d, bn), BF16)],
        compiler_params=pltpu.CompilerParams(
            dimension_semantics=("arbitrary", "arbitrary"), vmem_limit_bytes=_vmem_limit(est)),
        name="ffn_gate_up",
    )(xb, wg, wu, wd, conv_w, conv_b, state8)


def kernel(x_prompt, x_sample, state_mlstm_C, state_mlstm_n, state_mlstm_m, state_hgrn_S, state_ffn_conv,
           ln_g, ln_b, w_a_in, b_a_gate, a_norm_w, w_a_out, w_b_in, b_lower, b_norm_w, w_b_out,
           w_ffn_gate, w_ffn_up, w_ffn_down, ffn_conv_w, ffn_conv_b):
    bp, tp, d = x_prompt.shape
    bs, ts, _ = x_sample.shape
    mp = bp * tp
    ms = bs * ts
    m = mp + ms
    assert bp == 1 and ts == CHUNK and d == D_MODEL
    l_prompt = 128

    x = jnp.concatenate([x_prompt.reshape(mp, d), x_sample.reshape(ms, d)], axis=0)
    xb = x.astype(BF16)
    w_a_in_t = jnp.swapaxes(w_a_in, 1, 2)

    p_c, p_n, p_m, p_s, p_conv = [], [], [], [], []
    s_c, s_n, s_m, s_s, s_conv = [], [], [], [], []
    for l in range(DEPTH):
        j = l // 2
        if l % 2 == 0:
            w_gate = jnp.pad(w_a_in_t[j, A_MAIN:, :].T, ((0, 0), (0, LANES - 2 * NH_A))).astype(BF16)
            b_gate = jnp.pad(b_a_gate[j], (0, LANES - 2 * NH_A)).reshape(1, LANES)
            proj = matmul_wcast(xb, w_a_in_t, j, n_cols=A_MAIN, w_t=True, bm=544, bn=1024,
                                name="mlstm_in_proj")
            gates = mlstm_gates(xb, w_gate, b_gate)
            y_p, c_p, n_p, m_p = mlstm_scan(proj, gates, a_norm_w[j], row0=0, n_seq=1,
                                            n_chunk=mp // l_prompt, L=l_prompt)
            y_s, c_s2, n_s2, m_s2 = mlstm_scan(
                proj, gates, a_norm_w[j], row0=mp, n_seq=bs, n_chunk=1, L=ts,
                init=(state_mlstm_C, state_mlstm_n, state_mlstm_m, j))
            p_c.append(c_p); p_n.append(n_p); p_m.append(m_p)
            s_c.append(c_s2); s_n.append(n_s2); s_m.append(m_s2)
            y_mix = jnp.concatenate([y_p, y_s], axis=0)
            w_out = w_a_out
        else:
            proj = matmul_wcast(xb, w_b_in, j, bm=544, bn=1024, name="hgrn_in_proj")
            y_mix, st = hgrn_scan(proj, b_lower, b_norm_w[j], state_hgrn_S, layer_j=j,
                                  n_prompt_chunks=mp // CHUNK)
            p_s.append(st[:1]); s_s.append(st[1:])
            w_out = w_b_out
        y = matmul_wcast(y_mix, w_out, j, bm=544, bn=1024, res=x, alpha=ALPHA, name="mixer_out_proj")
        x, xb = layer_norm(y, ln_g[l, 0], ln_b[l, 0])

        st8 = jnp.pad(state_ffn_conv[l], ((0, 0), (SUBLANES - (CONV_W - 1), 0), (0, 0)))
        hmid, so, wd_b = ffn_gate_up(xb, w_ffn_gate, w_ffn_up, w_ffn_down, ffn_conv_w,
                                     ffn_conv_b.reshape(DEPTH, 1, D_FF), st8, l, bm=1088, bn=256, rb=272,
                                     n_prompt_rows=mp, seq_len=ts)
        conv_new = so[:, SUBLANES - (CONV_W - 1):, :]
        p_conv.append(conv_new[:1]); s_conv.append(conv_new[1:])
        y = matmul_xres(hmid, wd_b, x, bm=544, bn=256, alpha=ALPHA, name="ffn_down_proj")
        x, xb = layer_norm(y, ln_g[l, 1], ln_b[l, 1])

    y_prompt = x[:mp].reshape(bp, tp, d)
    y_sample = x[mp:].reshape(bs, ts, d)
    return (y_prompt, y_sample,
            jnp.stack(p_c), jnp.stack(p_n), jnp.stack(p_m), jnp.stack(p_s), jnp.stack(p_conv),
            jnp.stack(s_c), jnp.stack(s_n), jnp.stack(s_m), jnp.stack(s_s), jnp.stack(s_conv))
```

```python
import functools

import jax
import jax.numpy as jnp
from jax import lax
from jax.experimental import pallas as pl
from jax.experimental.pallas import tpu as pltpu

F32 = jnp.float32
BF16 = jnp.bfloat16

D_MODEL = 4096
DEPTH = 4
CHUNK = 64
NH_A = 8
DK_A = 256
DV_A = 512
QK_A = NH_A * DK_A
VD_A = NH_A * DV_A
A_MAIN = 2 * QK_A + 2 * VD_A
GATE_SOFTCAP = 15.0
NH_B = 32
DK_B = 128
DV_B = 128
D_FF = 11008
FFN_BT = 256
CONV_W = 3
ALPHA = (2 * DEPTH) ** 0.25
LN_EPS = 1e-5
HEAD_EPS = 1e-6

LANES = 128
SUBLANES = 8
VMEM_LIMIT_CAP = 58 * 1024 * 1024


def _vmem_limit(nbytes):
    return int(min(VMEM_LIMIT_CAP, nbytes * 1.2 + (6 << 20)))


def _nbytes(shape, dtype):
    n = 1
    for s in shape:
        n *= s
    return n * jnp.dtype(dtype).itemsize


def _finish(acc, r_ref, o_ref, alpha):
    if r_ref is not None:
        acc = alpha * r_ref[...] + acc
    o_ref[...] = acc.astype(o_ref.dtype)


def _mm_wcast_kernel(*refs, nn, nm, w_t, has_res, alpha):
    if has_res:
        x_ref, wc_ref, r_ref, o_ref, wb_ref = refs
    else:
        x_ref, wc_ref, o_ref, wb_ref = refs
        r_ref = None
    p = pl.program_id(0)
    i = pl.program_id(1)
    crows = wc_ref.shape[0]

    @pl.when(p < nn)
    def _():
        wb_ref[p % 2, pl.ds(pl.multiple_of(i * crows, crows), crows), :] = wc_ref[...].astype(BF16)

    @pl.when(p > 0)
    def _():
        wb = wb_ref[(p + 1) % 2]
        if w_t:
            acc = lax.dot_general(x_ref[...], wb, (((1,), (1,)), ((), ())), preferred_element_type=F32)
        else:
            acc = jnp.dot(x_ref[...], wb, preferred_element_type=F32)
        _finish(acc, r_ref, o_ref, alpha)


def matmul_wcast(x, w, layer, *, bm, bn, n_cols=None, w_t=False, res=None, alpha=1.0, out_dtype=F32,
                 name="mm"):
    m, kdim = x.shape
    n = w.shape[1 if w_t else 2] if n_cols is None else n_cols
    nm, nn = m // bm, n // bn
    crows = (bn if w_t else kdim) // nm
    assert m % bm == 0 and n % bn == 0 and (bn if w_t else kdim) % nm == 0 and crows % SUBLANES == 0
    row = lambda p, i: jnp.where(p > 0, i, 0)
    xi = lambda p, i: (row(p, i), 0)
    oi = lambda p, i: (row(p, i), jnp.maximum(p - 1, 0))
    chunk = lambda p, i: jnp.where(p < nn, i, nm - 1)
    tile = lambda p: jnp.minimum(p, nn - 1)
    if w_t:
        wspec = pl.BlockSpec((None, crows, kdim), lambda p, i: (layer, tile(p) * nm + chunk(p, i), 0))
        wb_shape = (2, bn, kdim)
    else:
        wspec = pl.BlockSpec((None, crows, bn), lambda p, i: (layer, chunk(p, i), tile(p)))
        wb_shape = (2, kdim, bn)
    in_specs = [pl.BlockSpec((bm, kdim), xi), wspec]
    args = [x, w]
    est = 2 * (_nbytes((bm, kdim), BF16) + _nbytes(wspec.block_shape[1:], F32) + _nbytes((bm, bn), out_dtype))
    est += _nbytes((bm, bn), F32) + _nbytes(wb_shape, BF16)
    if res is not None:
        in_specs.append(pl.BlockSpec((bm, bn), oi))
        args.append(res)
        est += 2 * _nbytes((bm, bn), res.dtype)
    return pl.pallas_call(
        functools.partial(_mm_wcast_kernel, nn=nn, nm=nm, w_t=w_t, has_res=res is not None, alpha=alpha),
        grid=(nn + 1, nm),
        in_specs=in_specs,
        out_specs=pl.BlockSpec((bm, bn), oi),
        out_shape=jax.ShapeDtypeStruct((m, n), out_dtype),
        scratch_shapes=[pltpu.VMEM(wb_shape, BF16)],
        compiler_params=pltpu.CompilerParams(
            dimension_semantics=("arbitrary", "arbitrary"), vmem_limit_bytes=_vmem_limit(est)),
        name=name,
    )(*args)


def _mm_kernel(x_ref, w_ref, r_ref, o_ref, *, alpha):
    _finish(jnp.dot(x_ref[...], w_ref[...], preferred_element_type=F32), r_ref, o_ref, alpha)


def matmul_xres(x, w, res, *, bm, bn, alpha, name="mm"):
    m, kdim = x.shape
    n = w.shape[1]
    assert m % bm == 0 and n % bn == 0
    oi = lambda i, j: (i, j)
    est = 2 * (_nbytes((bm, kdim), BF16) + _nbytes((kdim, bn), BF16) + 2 * _nbytes((bm, bn), F32))
    est += _nbytes((bm, bn), F32)
    return pl.pallas_call(
        functools.partial(_mm_kernel, alpha=alpha),
        grid=(m // bm, n // bn),
        in_specs=[pl.BlockSpec((bm, kdim), lambda i, j: (i, 0)),
                  pl.BlockSpec((kdim, bn), lambda i, j: (0, j)),
                  pl.BlockSpec((bm, bn), oi)],
        out_specs=pl.BlockSpec((bm, bn), oi),
        out_shape=jax.ShapeDtypeStruct((m, n), F32),
        compiler_params=pltpu.CompilerParams(
            dimension_semantics=("arbitrary", "arbitrary"), vmem_limit_bytes=_vmem_limit(est)),
        name=name,
    )(x, w, res)


def _ln_kernel(y_ref, g_ref, b_ref, o_ref, ob_ref):
    y = y_ref[...]
    mu = jnp.mean(y, axis=-1, keepdims=True)
    yc = y - mu
    var = jnp.mean(yc * yc, axis=-1, keepdims=True)
    out = yc * lax.rsqrt(var + LN_EPS) * g_ref[...] + b_ref[...]
    o_ref[...] = out
    ob_ref[...] = out.astype(BF16)


def layer_norm(y, g, b, *, bm=256):
    m, d = y.shape
    row = pl.BlockSpec((bm, d), lambda i: (i, 0))
    vec = pl.BlockSpec((1, d), lambda i: (0, 0))
    est = 2 * (2 * _nbytes((bm, d), F32) + _nbytes((bm, d), BF16)) + 2 * _nbytes((bm, d), F32)
    return pl.pallas_call(
        _ln_kernel,
        grid=(m // bm,),
        in_specs=[row, vec, vec],
        out_specs=[row, row],
        out_shape=[jax.ShapeDtypeStruct((m, d), F32), jax.ShapeDtypeStruct((m, d), BF16)],
        compiler_params=pltpu.CompilerParams(
            dimension_semantics=("arbitrary",), vmem_limit_bytes=_vmem_limit(est)),
        name="layer_norm",
    )(y, g.reshape(1, d), b.reshape(1, d))


def _log_sigmoid(z):
    return jnp.minimum(z, 0.0) - jnp.log1p(jnp.exp(-jnp.abs(z)))


def _gates_kernel(x_ref, w_ref, b_ref, o_ref):
    g = lax.dot_general(x_ref[...], w_ref[...], (((1,), (1,)), ((), ())), preferred_element_type=F32)
    g = g + b_ref[...]
    sc = GATE_SOFTCAP * jnp.tanh(g / GATE_SOFTCAP)
    lane = lax.broadcasted_iota(jnp.int32, sc.shape, 1)
    o_ref[...] = jnp.where(lane < NH_A, sc, _log_sigmoid(sc))


def mlstm_gates(xb, w_gate_t, b_gate, *, bm=1088):
    m, d = xb.shape
    est = 2 * (_nbytes((bm, d), BF16) + _nbytes((LANES, d), BF16) + _nbytes((bm, LANES), F32))
    return pl.pallas_call(
        _gates_kernel,
        grid=(m // bm,),
        in_specs=[pl.BlockSpec((bm, d), lambda i: (i, 0)),
                  pl.BlockSpec((LANES, d), lambda i: (0, 0)),
                  pl.BlockSpec((1, LANES), lambda i: (0, 0))],
        out_specs=pl.BlockSpec((bm, LANES), lambda i: (i, 0)),
        out_shape=jax.ShapeDtypeStruct((m, LANES), F32),
        compiler_params=pltpu.CompilerParams(
            dimension_semantics=("arbitrary",), vmem_limit_bytes=_vmem_limit(est)),
        name="mlstm_gates",
    )(xb, w_gate_t, b_gate)


def _mlstm_kernel(*refs, L, has_init):
    if has_init:
        (q_ref, k_ref, v_ref, o_ref, gc_ref, gr_ref, nw_ref, c0_ref, n0_ref, m0_ref,
         y_ref, ct_ref, nt_ref, mt_ref, c_s, n_s, m_s) = refs
    else:
        (q_ref, k_ref, v_ref, o_ref, gc_ref, gr_ref, nw_ref,
         y_ref, ct_ref, nt_ref, mt_ref, c_s, n_s, m_s) = refs
    c = pl.program_id(1)
    nc = pl.num_programs(1)

    @pl.when(c == 0)
    def _():
        if has_init:
            c_s[...] = c0_ref[0]
            n_s[...] = n0_ref[0]
            m_s[...] = m0_ref[0]
        else:
            c_s[...] = jnp.zeros_like(c_s)
            n_s[...] = jnp.zeros_like(n_s)
            m_s[...] = jnp.zeros_like(m_s)

    t_idx = lax.broadcasted_iota(jnp.int32, (L, L), 0)
    s_idx = lax.broadcasted_iota(jnp.int32, (L, L), 1)
    causal = s_idx <= t_idx
    lane = lax.broadcasted_iota(jnp.int32, (L, LANES), 1)
    gcol = gc_ref[...]

    def head(h, carry):
        qs = pl.ds(pl.multiple_of(h * DK_A, DK_A), DK_A)
        vs = pl.ds(pl.multiple_of(h * DV_A, DV_A), DV_A)
        q = q_ref[:, qs]
        k = k_ref[:, qs] * (DK_A ** -0.5)
        v = v_ref[:, vs]
        qb = q.astype(BF16)
        kb = k.astype(BF16)
        vb = v.astype(BF16)
        ig_r = gr_ref[h, 0]
        lf_r = gr_ref[NH_A + h, 0]
        ig_c = jnp.sum(jnp.where(lane == h, gcol, 0.0), axis=1, keepdims=True)
        lf_c = jnp.sum(jnp.where(lane == NH_A + h, gcol, 0.0), axis=1, keepdims=True)
        f_c = jnp.sum(jnp.where(causal, lf_r, 0.0), axis=1, keepdims=True)
        f_r = jnp.sum(jnp.where(t_idx <= s_idx, lf_c, 0.0), axis=0, keepdims=True)
        f_end = jnp.sum(lf_r, axis=1, keepdims=True)
        m_prev = m_s[h]
        c_prev = c_s[h]
        n_prev = n_s[h]

        d = jnp.where(causal, f_c - f_r + ig_r, -jnp.inf)
        g = f_c + m_prev
        m_t = jnp.maximum(g, jnp.max(d, axis=1, keepdims=True))
        w = jnp.exp(d - m_t)
        inter = jnp.exp(g - m_t)
        s = lax.dot_general(qb, kb, (((1,), (1,)), ((), ())), preferred_element_type=F32) * w
        num = jnp.dot(s.astype(BF16), vb, preferred_element_type=F32)
        num = num + inter * jnp.dot(qb, c_prev.astype(BF16), preferred_element_type=F32)
        den = jnp.sum(s, axis=1, keepdims=True) + inter * jnp.sum(q * n_prev, axis=1, keepdims=True)
        hh = num / jnp.maximum(jnp.abs(den), jnp.exp(-m_t))

        d_end_r = f_end - f_r + ig_r
        d_end_c = f_end - f_c + ig_c
        m_new = jnp.maximum(f_end + m_prev, jnp.max(d_end_r, axis=1, keepdims=True))
        w_end_c = jnp.exp(d_end_c - m_new)
        decay = jnp.exp(f_end + m_prev - m_new)
        kw = k * w_end_c
        c_s[h] = decay * c_prev + lax.dot_general(
            kw.astype(BF16), vb, (((0,), (0,)), ((), ())), preferred_element_type=F32)
        n_s[h] = decay * n_prev + jnp.sum(kw, axis=0, keepdims=True)
        m_s[h] = m_new

        mu = jnp.mean(hh, axis=1, keepdims=True)
        hc = hh - mu
        hn = hc * lax.rsqrt(jnp.mean(hc * hc, axis=1, keepdims=True) + HEAD_EPS)
        hn = hn * nw_ref[:, vs]
        y_ref[:, vs] = (jax.nn.sigmoid(o_ref[:, vs]) * hn).astype(BF16)
        return carry

    lax.fori_loop(0, NH_A, head, 0, unroll=2)

    @pl.when(c == nc - 1)
    def _():
        ct_ref[0] = c_s[...]
        nt_ref[0] = n_s[...]
        mt_ref[0] = m_s[...]


def mlstm_scan(proj, gates, norm_w, *, row0, n_seq, n_chunk, L, init=None):
    rb0 = row0 // L
    rows = n_seq * n_chunk * L
    g_rows = lax.slice_in_dim(gates, row0, row0 + rows, axis=0)
    g_t = g_rows[:, :2 * NH_A].T.reshape(2 * NH_A, n_seq * n_chunk, 1, L)
    rmap = lambda s, c: (rb0 + s * n_chunk + c)
    in_specs = [
        pl.BlockSpec((L, QK_A), lambda s, c: (rmap(s, c), 0)),
        pl.BlockSpec((L, QK_A), lambda s, c: (rmap(s, c), 1)),
        pl.BlockSpec((L, VD_A), lambda s, c: (rmap(s, c), 1)),
        pl.BlockSpec((L, VD_A), lambda s, c: (rmap(s, c), 2)),
        pl.BlockSpec((L, LANES), lambda s, c: (rmap(s, c), 0)),
        pl.BlockSpec((2 * NH_A, 1, 1, L), lambda s, c: (0, s * n_chunk + c, 0, 0)),
        pl.BlockSpec((1, VD_A), lambda s, c: (0, 0)),
    ]
    args = [proj, proj, proj, proj, gates, g_t, norm_w.reshape(1, VD_A)]
    state_specs = [
        pl.BlockSpec((1, NH_A, DK_A, DV_A), lambda s, c: (s, 0, 0, 0)),
        pl.BlockSpec((1, NH_A, 1, DK_A), lambda s, c: (s, 0, 0, 0)),
        pl.BlockSpec((1, NH_A, 1, 1), lambda s, c: (s, 0, 0, 0)),
    ]
    if init is not None:
        c0, n0, m0, layer = init
        nl = c0.shape[0]
        in_specs += [
            pl.BlockSpec((None, 1, NH_A, DK_A, DV_A), lambda s, c: (layer, s, 0, 0, 0)),
            pl.BlockSpec((None, 1, NH_A, 1, DK_A), lambda s, c: (layer, s, 0, 0, 0)),
            pl.BlockSpec((None, 1, NH_A, 1, 1), lambda s, c: (layer, s, 0, 0, 0)),
        ]
        args += [c0, n0.reshape(nl, n_seq, NH_A, 1, DK_A), m0.reshape(nl, n_seq, NH_A, 1, 1)]
    est = 2 * (2 * _nbytes((L, QK_A), F32) + 2 * _nbytes((L, VD_A), F32) + _nbytes((L, VD_A), BF16))
    est += (3 if init is None else 5) * _nbytes((NH_A, DK_A, DV_A), F32)
    est += 8 * _nbytes((L, max(L, DV_A)), F32)
    y, ct, nt, mt = pl.pallas_call(
        functools.partial(_mlstm_kernel, L=L, has_init=init is not None),
        grid=(n_seq, n_chunk),
        in_specs=in_specs,
        out_specs=[pl.BlockSpec((L, VD_A), lambda s, c: (s * n_chunk + c, 0))] + state_specs,
        out_shape=[
            jax.ShapeDtypeStruct((rows, VD_A), BF16),
            jax.ShapeDtypeStruct((n_seq, NH_A, DK_A, DV_A), F32),
            jax.ShapeDtypeStruct((n_seq, NH_A, 1, DK_A), F32),
            jax.ShapeDtypeStruct((n_seq, NH_A, 1, 1), F32),
        ],
        scratch_shapes=[
            pltpu.VMEM((NH_A, DK_A, DV_A), F32),
            pltpu.VMEM((NH_A, 1, DK_A), F32),
            pltpu.VMEM((NH_A, 1, 1), F32),
        ],
        compiler_params=pltpu.CompilerParams(
            dimension_semantics=("arbitrary", "arbitrary"), vmem_limit_bytes=_vmem_limit(est)),
        name="mlstm_scan_init" if init is not None else "mlstm_scan",
    )(*args)
    return y, ct, nt.reshape(n_seq, NH_A, DK_A), mt.reshape(n_seq, NH_A)


def _hgrn_kernel(q_ref, f_ref, i_ref, g_ref, bl_ref, nw_ref, s0_ref, y_ref, st_ref, s_s,
                 *, layer_j, n_prompt_chunks, head_unroll):
    L = CHUNK
    g_id = pl.program_id(0)

    @pl.when(g_id == 0)
    def _():
        s_s[...] = jnp.zeros_like(s_s)

    @pl.when(g_id >= n_prompt_chunks)
    def _():
        def load(h, carry):
            s_s[h] = s0_ref[0, h].T
            return carry
        lax.fori_loop(0, NH_B, load, 0)

    row = lax.broadcasted_iota(jnp.int32, (L, DK_B), 0)
    t_idx = lax.broadcasted_iota(jnp.int32, (L, L), 0)
    s_idx = lax.broadcasted_iota(jnp.int32, (L, L), 1)
    levels = [1, 2, 4, 8, 16, 32]
    upper = {mm: (row & mm) != 0 for mm in levels}
    pair = {mm: ((t_idx ^ s_idx) < 2 * mm) & ((t_idx & mm) != 0) & ((s_idx & mm) == 0) for mm in levels}
    diag = t_idx == s_idx
    nt_dims = (((1,), (1,)), ((), ()))
    tn_dims = (((0,), (0,)), ((), ()))

    def head(h, carry):
        sl = pl.ds(pl.multiple_of(h * DK_B, DK_B), DK_B)
        q = q_ref[:, sl]
        fz = f_ref[:, sl]
        inp = i_ref[:, sl]
        gate = g_ref[:, sl]
        bl = bl_ref[:, sl]
        e = jnp.exp(bl - jnp.max(bl, axis=0, keepdims=True))
        p = e / jnp.sum(e, axis=0, keepdims=True)
        lb = jnp.zeros((1, DK_B), F32)
        for r in range(1, layer_j + 1):
            lb = lb + p[r:r + 1]
        sig = jax.nn.sigmoid(fz)
        oml = 1.0 - lb
        logf = jnp.log(lb + oml * sig)
        k = oml * (1.0 - sig)

        ib = inp.astype(BF16)
        s_acc = jnp.where(
            diag, lax.dot_general(q.astype(BF16), k.astype(BF16), nt_dims, preferred_element_type=F32), 0.0)
        p_sum = logf
        r_sum = jnp.zeros_like(logf)
        g_sum = logf
        for mm in levels:
            up = upper[mm]
            sib = jnp.where(up, pltpu.roll(g_sum, mm, 0), pltpu.roll(g_sum, L - mm, 0))
            x = (jnp.where(up, q, k) * jnp.exp(jnp.where(up, p_sum, r_sum))).astype(BF16)
            s_m = lax.dot_general(x, x, nt_dims, preferred_element_type=F32)
            s_acc = jnp.where(pair[mm], s_m, s_acc)
            p_sum = p_sum + jnp.where(up, sib, 0.0)
            r_sum = r_sum + jnp.where(up, 0.0, sib)
            g_sum = g_sum + sib
        s_prev = s_s[h]
        o = lax.dot_general((q * jnp.exp(p_sum)).astype(BF16), s_prev.astype(BF16), nt_dims,
                            preferred_element_type=F32)
        o = o + jnp.dot(s_acc.astype(BF16), ib, preferred_element_type=F32)
        k_end = (k * jnp.exp(r_sum)).astype(BF16)
        s_s[h] = jnp.exp(g_sum[0:1]) * s_prev + lax.dot_general(ib, k_end, tn_dims, preferred_element_type=F32)

        o = o * lax.rsqrt(jnp.mean(o * o, axis=1, keepdims=True) + HEAD_EPS)
        o = o * nw_ref[:, sl] * (gate * jax.nn.sigmoid(gate))
        y_ref[:, sl] = o.astype(BF16)
        return carry

    lax.fori_loop(0, NH_B, head, 0, unroll=head_unroll)

    @pl.when(g_id >= n_prompt_chunks - 1)
    def _():
        def store(h, carry):
            st_ref[0, h] = s_s[h].T
            return carry
        lax.fori_loop(0, NH_B, store, 0)


def hgrn_scan(proj, b_lower, norm_w, s0, *, layer_j, n_prompt_chunks, head_unroll=8):
    m = proj.shape[0]
    n_chunks = m // CHUNK
    n_seq = n_chunks - n_prompt_chunks + 1
    n_b = b_lower.shape[0]
    col = lambda cb: pl.BlockSpec((CHUNK, D_MODEL), lambda g: (g, cb))
    est = 2 * (4 * _nbytes((CHUNK, D_MODEL), F32) + _nbytes((CHUNK, D_MODEL), BF16))
    est += 5 * _nbytes((NH_B, DK_B, DV_B), F32)
    y, st = pl.pallas_call(
        functools.partial(_hgrn_kernel, layer_j=layer_j, n_prompt_chunks=n_prompt_chunks,
                          head_unroll=head_unroll),
        grid=(n_chunks,),
        in_specs=[col(0), col(1), col(2), col(3),
                  pl.BlockSpec((n_b, D_MODEL), lambda g: (0, 0)),
                  pl.BlockSpec((1, D_MODEL), lambda g: (0, 0)),
                  pl.BlockSpec((None, 1, NH_B, DK_B, DV_B),
                               lambda g: (layer_j, jnp.maximum(g - n_prompt_chunks, 0), 0, 0, 0))],
        out_specs=[pl.BlockSpec((CHUNK, D_MODEL), lambda g: (g, 0)),
                   pl.BlockSpec((1, NH_B, DK_B, DV_B),
                                lambda g: (jnp.maximum(g - (n_prompt_chunks - 1), 0), 0, 0, 0))],
        out_shape=[jax.ShapeDtypeStruct((m, D_MODEL), BF16),
                   jax.ShapeDtypeStruct((n_seq, NH_B, DK_B, DV_B), F32)],
        scratch_shapes=[pltpu.VMEM((NH_B, DV_B, DK_B), F32)],
        compiler_params=pltpu.CompilerParams(
            dimension_semantics=("arbitrary",), vmem_limit_bytes=_vmem_limit(est)),
        name="hgrn_scan",
    )(proj, proj, proj, proj, b_lower, norm_w.reshape(1, D_MODEL), s0)
    return y, st


def _ffn_kernel(x_ref, wga_ref, wgb_ref, wua_ref, wub_ref, wdc_ref, cw_ref, cb_ref, st_ref,
                h_ref, so_ref, wdo_ref, carry_ref, wgs_ref, wus_ref,
                *, nm, nq, bt, rb, last_half_valid, wd_valid_blocks, prompt_end, sample_starts, seq_len):
    p = pl.program_id(0)
    i = pl.program_id(1)
    bm = x_ref.shape[0]
    crows = wga_ref.shape[0]

    def cast_next_pair():
        rows = pl.ds(pl.multiple_of(i * crows, crows), crows)
        for half, (g_ref, u_ref) in enumerate(((wga_ref, wua_ref), (wgb_ref, wub_ref))):
            wgs_ref[p % 2, half, rows, :] = g_ref[...].astype(BF16)
            wus_ref[p % 2, half, rows, :] = u_ref[...].astype(BF16)

    @pl.when(p == 0)
    def _():
        cast_next_pair()

    @pl.when(p > 0)
    def _():
        slot = (p + 1) % 2
        rows8 = lax.broadcasted_iota(jnp.int32, (SUBLANES, bt), 0)
        is_last = i == nm - 1
        drop = [None, None if last_half_valid else (p == nq)]
        cols = [slice(half * bt, (half + 1) * bt) for half in range(2)]
        w0 = [cw_ref[0:1, c] for c in cols]
        w1 = [cw_ref[1:2, c] for c in cols]
        w2 = [cw_ref[2:3, c] for c in cols]
        cb = [cb_ref[:, c] for c in cols]

        def act(half, a_cur, a_p1, a_p2, u_cur):
            c = cb[half] + a_p2 * w0[half]
            c = c + a_p1 * w1[half]
            c = c + a_cur * w2[half]
            out = c * jax.nn.sigmoid(c) * u_cur
            if drop[half] is not None:
                out = jnp.where(drop[half], 0.0, out)
            return out.astype(BF16)

        prev8 = [jnp.where(i == 0, 0.0, carry_ref[:, c]) for c in cols]
        for r0 in range(0, bm, rb):
            xs = x_ref[r0:r0 + rb, :]
            for half in range(2):
                c = cols[half]
                a = jnp.dot(xs, wgs_ref[slot, half], preferred_element_type=F32)
                u = jnp.dot(xs, wus_ref[slot, half], preferred_element_type=F32)
                h_ref[r0:r0 + rb, c] = act(half, a, pltpu.roll(a, 1, 0), pltpu.roll(a, 2, 0), u)

                def patch(off, before8):
                    blk = a[off:off + SUBLANES]
                    p1 = jnp.where(rows8 < 1, pltpu.roll(before8, 1, 0), pltpu.roll(blk, 1, 0))
                    p2 = jnp.where(rows8 < 2, pltpu.roll(before8, 2, 0), pltpu.roll(blk, 2, 0))
                    h_ref[r0 + off:r0 + off + SUBLANES, c] = act(half, blk, p1, p2, u[off:off + SUBLANES])

                patch(0, prev8[half])
                for s, start in enumerate(sample_starts):
                    off = start - r0
                    if SUBLANES <= off < rb:
                        patch(off, jnp.where(is_last, st_ref[s, :, c], a[off - SUBLANES:off]))
                    end = off + seq_len
                    if SUBLANES <= end <= rb:
                        so_ref[1 + s, :, c] = a[end - SUBLANES:end]
                if SUBLANES <= prompt_end - r0 <= rb:
                    so_ref[0, :, c] = a[prompt_end - r0 - SUBLANES:prompt_end - r0]
                prev8[half] = a[rb - SUBLANES:rb]
        for half in range(2):
            carry_ref[:, cols[half]] = prev8[half]

        cast_next_pair()
        wd_rows_valid = (p - 1) * nm + i < wd_valid_blocks
        wdo_ref[...] = jnp.where(wd_rows_valid, wdc_ref[...], 0.0).astype(BF16)


def ffn_gate_up(xb, wg, wu, wd, conv_w, conv_b, state8, layer, *, bm, bt, rb, n_prompt_rows, seq_len):
    m, d = xb.shape
    n = wg.shape[2]
    nm, n_tiles = m // bm, n // bt
    nq = -(-n_tiles // 2)
    n_pad = nq * 2 * bt
    crows = d // nm
    drows = 2 * bt // nm
    n_seq = state8.shape[0]
    tile0 = (nm - 1) * bm
    prompt_end = n_prompt_rows - tile0
    sample_starts = tuple(prompt_end + s * seq_len for s in range(n_seq))
    assert m % bm == 0 and n % bt == 0 and bm % rb == 0 and rb % SUBLANES == 0
    assert d % nm == 0 and crows % SUBLANES == 0 and (2 * bt) % nm == 0 and drows % SUBLANES == 0
    assert n % drows == 0 and conv_w.shape[2] == n_pad and conv_b.shape[2] == n_pad and state8.shape[2] == n_pad
    assert m == n_prompt_rows + n_seq * seq_len and SUBLANES <= prompt_end
    assert all(st % SUBLANES == 0 and st % rb != 0 for st in sample_starts)
    est = 2 * (_nbytes((bm, d), BF16) + 4 * _nbytes((crows, bt), F32) + _nbytes((bm, 2 * bt), BF16))
    est += 2 * (_nbytes((drows, d), F32) + _nbytes((drows, d), BF16))
    est += 8 * _nbytes((d, bt), BF16)
    est += 6 * _nbytes((rb, bt), F32) + 4 * _nbytes((n_seq + 1, SUBLANES, 2 * bt), F32)
    row = lambda p, i: jnp.where(p > 0, i, 0)
    jq = lambda p: jnp.maximum(p - 1, 0)
    chunk = lambda p, i: jnp.where(p < nq, i, nm - 1)
    tile_a = lambda p: 2 * jnp.minimum(p, nq - 1)
    tile_b = lambda p: jnp.minimum(tile_a(p) + 1, n_tiles - 1)
    wspec_a = pl.BlockSpec((None, crows, bt), lambda p, i: (layer, chunk(p, i), tile_a(p)))
    wspec_b = pl.BlockSpec((None, crows, bt), lambda p, i: (layer, chunk(p, i), tile_b(p)))
    dblock = lambda p, i: jq(p) * nm + row(p, i)
    wd_valid_blocks = n // drows
    pair = lambda p, i: (layer, 0, jq(p))
    return pl.pallas_call(
        functools.partial(_ffn_kernel, nm=nm, nq=nq, bt=bt, rb=rb, last_half_valid=n_tiles % 2 == 0,
                          wd_valid_blocks=wd_valid_blocks, prompt_end=prompt_end,
                          sample_starts=sample_starts, seq_len=seq_len),
        grid=(nq + 1, nm),
        in_specs=[pl.BlockSpec((bm, d), lambda p, i: (row(p, i), 0)),
                  wspec_a, wspec_b, wspec_a, wspec_b,
                  pl.BlockSpec((None, drows, d),
                               lambda p, i: (layer, jnp.minimum(dblock(p, i), wd_valid_blocks - 1), 0)),
                  pl.BlockSpec((None, CONV_W, 2 * bt), pair),
                  pl.BlockSpec((None, 1, 2 * bt), pair),
                  pl.BlockSpec((n_seq, SUBLANES, 2 * bt), lambda p, i: (0, 0, jq(p)))],
        out_specs=[pl.BlockSpec((bm, 2 * bt), lambda p, i: (row(p, i), jq(p))),
                   pl.BlockSpec((n_seq + 1, SUBLANES, 2 * bt), lambda p, i: (0, 0, jq(p))),
                   pl.BlockSpec((drows, d), lambda p, i: (dblock(p, i), 0))],
        out_shape=[jax.ShapeDtypeStruct((m, n_pad), BF16),
                   jax.ShapeDtypeStruct((n_seq + 1, SUBLANES, n_pad), F32),
                   jax.ShapeDtypeStruct((n_pad, d), BF16)],
        scratch_shapes=[pltpu.VMEM((SUBLANES, 2 * bt), F32),
                        pltpu.VMEM((2, 2, d, bt), BF16),
                        pltpu.VMEM((2, 2, d, bt), BF16)],
        compiler_params=pltpu.CompilerParams(
            dimension_semantics=("arbitrary", "arbitrary"), vmem_limit_bytes=_vmem_limit(est)),
        name="ffn_gate_up",
    )(xb, wg, wg, wu, wu, wd, conv_w, conv_b, state8)


def kernel(x_prompt, x_sample, state_mlstm_C, state_mlstm_n, state_mlstm_m, state_hgrn_S, state_ffn_conv,
           ln_g, ln_b, w_a_in, b_a_gate, a_norm_w, w_a_out, w_b_in, b_lower, b_norm_w, w_b_out,
           w_ffn_gate, w_ffn_up, w_ffn_down, ffn_conv_w, ffn_conv_b):
    bp, tp, d = x_prompt.shape
    bs, ts, _ = x_sample.shape
    mp = bp * tp
    ms = bs * ts
    m = mp + ms
    assert bp == 1 and ts == CHUNK and d == D_MODEL
    l_prompt = 128

    x = jnp.concatenate([x_prompt.reshape(mp, d), x_sample.reshape(ms, d)], axis=0)
    xb = x.astype(BF16)
    w_a_in_t = jnp.swapaxes(w_a_in, 1, 2)
    ff_pad = -D_FF % (2 * FFN_BT)
    conv_w_p = jnp.pad(ffn_conv_w, ((0, 0), (0, 0), (0, ff_pad)))
    conv_b_p = jnp.pad(ffn_conv_b, ((0, 0), (0, ff_pad))).reshape(DEPTH, 1, D_FF + ff_pad)

    p_c, p_n, p_m, p_s, p_conv = [], [], [], [], []
    s_c, s_n, s_m, s_s, s_conv = [], [], [], [], []
    for l in range(DEPTH):
        j = l // 2
        if l % 2 == 0:
            w_gate = jnp.pad(w_a_in_t[j, A_MAIN:, :], ((0, LANES - 2 * NH_A), (0, 0))).astype(BF16)
            b_gate = jnp.pad(b_a_gate[j], (0, LANES - 2 * NH_A)).reshape(1, LANES)
            proj = matmul_wcast(xb, w_a_in_t, j, n_cols=A_MAIN, w_t=True, bm=1088, bn=1024,
                                name="mlstm_in_proj")
            gates = mlstm_gates(xb, w_gate, b_gate)
            y_p, c_p, n_p, m_p = mlstm_scan(proj, gates, a_norm_w[j], row0=0, n_seq=1,
                                            n_chunk=mp // l_prompt, L=l_prompt)
            y_s, c_s2, n_s2, m_s2 = mlstm_scan(
                proj, gates, a_norm_w[j], row0=mp, n_seq=bs, n_chunk=1, L=ts,
                init=(state_mlstm_C, state_mlstm_n, state_mlstm_m, j))
            p_c.append(c_p); p_n.append(n_p); p_m.append(m_p)
            s_c.append(c_s2); s_n.append(n_s2); s_m.append(m_s2)
            y_mix = jnp.concatenate([y_p, y_s], axis=0)
            w_out = w_a_out
        else:
            proj = matmul_wcast(xb, w_b_in, j, bm=1088, bn=1024, name="hgrn_in_proj")
            y_mix, st = hgrn_scan(proj, b_lower, b_norm_w[j], state_hgrn_S, layer_j=j,
                                  n_prompt_chunks=mp // CHUNK)
            p_s.append(st[:1]); s_s.append(st[1:])
            w_out = w_b_out
        y = matmul_wcast(y_mix, w_out, j, bm=544, bn=1024, res=x, alpha=ALPHA, name="mixer_out_proj")
        x, xb = layer_norm(y, ln_g[l, 0], ln_b[l, 0])

        st8 = jnp.pad(state_ffn_conv[l], ((0, 0), (SUBLANES - (CONV_W - 1), 0), (0, ff_pad)))
        hmid, so, wd_b = ffn_gate_up(xb, w_ffn_gate, w_ffn_up, w_ffn_down, conv_w_p, conv_b_p, st8, l,
                                     bm=1088, bt=FFN_BT, rb=272, n_prompt_rows=mp, seq_len=ts)
        conv_new = so[:, SUBLANES - (CONV_W - 1):, :D_FF]
        p_conv.append(conv_new[:1]); s_conv.append(conv_new[1:])
        y = matmul_xres(hmid, wd_b, x, bm=544, bn=512, alpha=ALPHA, name="ffn_down_proj")
        x, xb = layer_norm(y, ln_g[l, 1], ln_b[l, 1])

    y_prompt = x[:mp].reshape(bp, tp, d)
    y_sample = x[mp:].reshape(bs, ts, d)
    return (y_prompt, y_sample,
            jnp.stack(p_c), jnp.stack(p_n), jnp.stack(p_m), jnp.stack(p_s), jnp.stack(p_conv),
            jnp.stack(s_c), jnp.stack(s_n), jnp.stack(s_m), jnp.stack(s_s), jnp.stack(s_conv))
```

```python
import functools

import jax
import jax.numpy as jnp
from jax import lax
from jax.experimental import pallas as pl
from jax.experimental.pallas import tpu as pltpu

F32 = jnp.float32
BF16 = jnp.bfloat16

D_MODEL = 4096
DEPTH = 4
CHUNK = 64
NH_A = 8
DK_A = 256
DV_A = 512
QK_A = NH_A * DK_A
VD_A = NH_A * DV_A
A_MAIN = 2 * QK_A + 2 * VD_A
GATE_SOFTCAP = 15.0
NH_B = 32
DK_B = 128
DV_B = 128
D_FF = 11008
FFN_BT = 256
CONV_W = 3
ALPHA = (2 * DEPTH) ** 0.25
LN_EPS = 1e-5
HEAD_EPS = 1e-6

LANES = 128
SUBLANES = 8
VMEM_LIMIT_CAP = 58 * 1024 * 1024


def _vmem_limit(nbytes):
    return int(min(VMEM_LIMIT_CAP, nbytes * 1.2 + (6 << 20)))


def _nbytes(shape, dtype):
    n = 1
    for s in shape:
        n *= s
    return n * jnp.dtype(dtype).itemsize


def _finish(acc, r_ref, o_ref, alpha):
    if r_ref is not None:
        acc = alpha * r_ref[...] + acc
    o_ref[...] = acc.astype(o_ref.dtype)


def _mm_wcast_kernel(*refs, nn, nm, w_t, has_res, alpha):
    if has_res:
        x_ref, wc_ref, r_ref, o_ref, wb_ref = refs
    else:
        x_ref, wc_ref, o_ref, wb_ref = refs
        r_ref = None
    p = pl.program_id(0)
    i = pl.program_id(1)
    crows = wc_ref.shape[0]

    @pl.when(p < nn)
    def _():
        wb_ref[p % 2, pl.ds(pl.multiple_of(i * crows, crows), crows), :] = wc_ref[...].astype(BF16)

    @pl.when(p > 0)
    def _():
        wb = wb_ref[(p + 1) % 2]
        if w_t:
            acc = lax.dot_general(x_ref[...], wb, (((1,), (1,)), ((), ())), preferred_element_type=F32)
        else:
            acc = jnp.dot(x_ref[...], wb, preferred_element_type=F32)
        _finish(acc, r_ref, o_ref, alpha)


def matmul_wcast(x, w, layer, *, bm, bn, n_cols=None, w_t=False, res=None, alpha=1.0, out_dtype=F32,
                 name="mm"):
    m, kdim = x.shape
    n = w.shape[1 if w_t else 2] if n_cols is None else n_cols
    nm, nn = m // bm, n // bn
    crows = (bn if w_t else kdim) // nm
    assert m % bm == 0 and n % bn == 0 and (bn if w_t else kdim) % nm == 0 and crows % SUBLANES == 0
    row = lambda p, i: jnp.where(p > 0, i, 0)
    xi = lambda p, i: (row(p, i), 0)
    oi = lambda p, i: (row(p, i), jnp.maximum(p - 1, 0))
    chunk = lambda p, i: jnp.where(p < nn, i, nm - 1)
    tile = lambda p: jnp.minimum(p, nn - 1)
    if w_t:
        wspec = pl.BlockSpec((None, crows, kdim), lambda p, i: (layer, tile(p) * nm + chunk(p, i), 0))
        wb_shape = (2, bn, kdim)
    else:
        wspec = pl.BlockSpec((None, crows, bn), lambda p, i: (layer, chunk(p, i), tile(p)))
        wb_shape = (2, kdim, bn)
    in_specs = [pl.BlockSpec((bm, kdim), xi), wspec]
    args = [x, w]
    est = 2 * (_nbytes((bm, kdim), BF16) + _nbytes(wspec.block_shape[1:], F32) + _nbytes((bm, bn), out_dtype))
    est += _nbytes((bm, bn), F32) + _nbytes(wb_shape, BF16)
    if res is not None:
        in_specs.append(pl.BlockSpec((bm, bn), oi))
        args.append(res)
        est += 2 * _nbytes((bm, bn), res.dtype)
    return pl.pallas_call(
        functools.partial(_mm_wcast_kernel, nn=nn, nm=nm, w_t=w_t, has_res=res is not None, alpha=alpha),
        grid=(nn + 1, nm),
        in_specs=in_specs,
        out_specs=pl.BlockSpec((bm, bn), oi),
        out_shape=jax.ShapeDtypeStruct((m, n), out_dtype),
        scratch_shapes=[pltpu.VMEM(wb_shape, BF16)],
        compiler_params=pltpu.CompilerParams(
            dimension_semantics=("arbitrary", "arbitrary"), vmem_limit_bytes=_vmem_limit(est)),
        name=name,
    )(*args)


def _mm_kernel(x_ref, w_ref, r_ref, o_ref, *, alpha):
    _finish(jnp.dot(x_ref[...], w_ref[...], preferred_element_type=F32), r_ref, o_ref, alpha)


def matmul_xres(x, w, res, *, bm, bn, alpha, name="mm"):
    m, kdim = x.shape
    n = w.shape[1]
    assert m % bm == 0 and n % bn == 0
    oi = lambda i, j: (i, j)
    est = 2 * (_nbytes((bm, kdim), BF16) + _nbytes((kdim, bn), BF16) + 2 * _nbytes((bm, bn), F32))
    est += _nbytes((bm, bn), F32)
    return pl.pallas_call(
        functools.partial(_mm_kernel, alpha=alpha),
        grid=(m // bm, n // bn),
        in_specs=[pl.BlockSpec((bm, kdim), lambda i, j: (i, 0)),
                  pl.BlockSpec((kdim, bn), lambda i, j: (0, j)),
                  pl.BlockSpec((bm, bn), oi)],
        out_specs=pl.BlockSpec((bm, bn), oi),
        out_shape=jax.ShapeDtypeStruct((m, n), F32),
        compiler_params=pltpu.CompilerParams(
            dimension_semantics=("arbitrary", "arbitrary"), vmem_limit_bytes=_vmem_limit(est)),
        name=name,
    )(x, w, res)


def _ln_kernel(y_ref, g_ref, b_ref, o_ref, ob_ref=None):
    y = y_ref[...]
    mu = jnp.mean(y, axis=-1, keepdims=True)
    yc = y - mu
    var = jnp.mean(yc * yc, axis=-1, keepdims=True)
    out = yc * lax.rsqrt(var + LN_EPS) * g_ref[...] + b_ref[...]
    o_ref[...] = out
    if ob_ref is not None:
        ob_ref[...] = out.astype(BF16)


def layer_norm(y, g, b, *, bm=256, row0=0, rows=None, emit_bf16=True):
    m, d = y.shape
    rows = m - row0 if rows is None else rows
    assert row0 % bm == 0 and rows % bm == 0
    blk0 = row0 // bm
    row = pl.BlockSpec((bm, d), lambda i: (i, 0))
    vec = pl.BlockSpec((1, d), lambda i: (0, 0))
    est = 2 * (2 * _nbytes((bm, d), F32) + _nbytes((bm, d), BF16)) + 2 * _nbytes((bm, d), F32)
    out_specs = [row, row] if emit_bf16 else [row]
    out_shape = [jax.ShapeDtypeStruct((rows, d), F32)]
    if emit_bf16:
        out_shape.append(jax.ShapeDtypeStruct((rows, d), BF16))
    return pl.pallas_call(
        _ln_kernel,
        grid=(rows // bm,),
        in_specs=[pl.BlockSpec((bm, d), lambda i: (blk0 + i, 0)), vec, vec],
        out_specs=out_specs,
        out_shape=out_shape,
        compiler_params=pltpu.CompilerParams(
            dimension_semantics=("arbitrary",), vmem_limit_bytes=_vmem_limit(est)),
        name="layer_norm",
    )(y, g.reshape(1, d), b.reshape(1, d))


def _log_sigmoid(z):
    return jnp.minimum(z, 0.0) - jnp.log1p(jnp.exp(-jnp.abs(z)))


def _gates_kernel(x_ref, w_ref, b_ref, o_ref):
    g = lax.dot_general(x_ref[...], w_ref[...].astype(BF16), (((1,), (1,)), ((), ())),
                        preferred_element_type=F32)
    g = g + b_ref[...]
    sc = GATE_SOFTCAP * jnp.tanh(g / GATE_SOFTCAP)
    lane = lax.broadcasted_iota(jnp.int32, sc.shape, 1)
    o_ref[...] = jnp.where(lane < NH_A, sc, _log_sigmoid(sc))


def mlstm_gates(xb, w_gate_t, b_gate, *, bm=1088):
    m, d = xb.shape
    est = 2 * (_nbytes((bm, d), BF16) + 2 * _nbytes((LANES, d), F32) + _nbytes((bm, LANES), F32))
    return pl.pallas_call(
        _gates_kernel,
        grid=(m // bm,),
        in_specs=[pl.BlockSpec((bm, d), lambda i: (i, 0)),
                  pl.BlockSpec((LANES, d), lambda i: (0, 0)),
                  pl.BlockSpec((1, LANES), lambda i: (0, 0))],
        out_specs=pl.BlockSpec((bm, LANES), lambda i: (i, 0)),
        out_shape=jax.ShapeDtypeStruct((m, LANES), F32),
        compiler_params=pltpu.CompilerParams(
            dimension_semantics=("arbitrary",), vmem_limit_bytes=_vmem_limit(est)),
        name="mlstm_gates",
    )(xb, w_gate_t, b_gate)


def _mlstm_kernel(*refs, L, has_init):
    if has_init:
        (q_ref, k_ref, v_ref, o_ref, gc_ref, gr_ref, nw_ref, c0_ref, n0_ref, m0_ref,
         y_ref, ct_ref, nt_ref, mt_ref, c_s, n_s, m_s) = refs
    else:
        (q_ref, k_ref, v_ref, o_ref, gc_ref, gr_ref, nw_ref,
         y_ref, ct_ref, nt_ref, mt_ref, c_s, n_s, m_s) = refs
    c = pl.program_id(1)
    nc = pl.num_programs(1)

    @pl.when(c == 0)
    def _():
        if has_init:
            c_s[...] = c0_ref[0]
            n_s[...] = n0_ref[0]
            m_s[...] = m0_ref[0]
        else:
            c_s[...] = jnp.zeros_like(c_s)
            n_s[...] = jnp.zeros_like(n_s)
            m_s[...] = jnp.zeros_like(m_s)

    t_idx = lax.broadcasted_iota(jnp.int32, (L, L), 0)
    s_idx = lax.broadcasted_iota(jnp.int32, (L, L), 1)
    causal = s_idx <= t_idx
    lane = lax.broadcasted_iota(jnp.int32, (L, LANES), 1)
    gcol = gc_ref[...]

    def head(h, carry):
        qs = pl.ds(pl.multiple_of(h * DK_A, DK_A), DK_A)
        vs = pl.ds(pl.multiple_of(h * DV_A, DV_A), DV_A)
        q = q_ref[:, qs]
        k = k_ref[:, qs] * (DK_A ** -0.5)
        v = v_ref[:, vs]
        qb = q.astype(BF16)
        kb = k.astype(BF16)
        vb = v.astype(BF16)
        ig_r = gr_ref[h, 0]
        lf_r = gr_ref[NH_A + h, 0]
        ig_c = jnp.sum(jnp.where(lane == h, gcol, 0.0), axis=1, keepdims=True)
        lf_c = jnp.sum(jnp.where(lane == NH_A + h, gcol, 0.0), axis=1, keepdims=True)
        f_c = jnp.sum(jnp.where(causal, lf_r, 0.0), axis=1, keepdims=True)
        f_r = jnp.sum(jnp.where(t_idx <= s_idx, lf_c, 0.0), axis=0, keepdims=True)
        f_end = jnp.sum(lf_r, axis=1, keepdims=True)
        m_prev = m_s[h]
        c_prev = c_s[h]
        n_prev = n_s[h]

        d = jnp.where(causal, f_c - f_r + ig_r, -jnp.inf)
        g = f_c + m_prev
        m_t = jnp.maximum(g, jnp.max(d, axis=1, keepdims=True))
        w = jnp.exp(d - m_t)
        inter = jnp.exp(g - m_t)
        s = lax.dot_general(qb, kb, (((1,), (1,)), ((), ())), preferred_element_type=F32) * w
        num = jnp.dot(s.astype(BF16), vb, preferred_element_type=F32)
        num = num + inter * jnp.dot(qb, c_prev.astype(BF16), preferred_element_type=F32)
        den = jnp.sum(s, axis=1, keepdims=True) + inter * jnp.sum(q * n_prev, axis=1, keepdims=True)
        hh = num / jnp.maximum(jnp.abs(den), jnp.exp(-m_t))

        d_end_r = f_end - f_r + ig_r
        d_end_c = f_end - f_c + ig_c
        m_new = jnp.maximum(f_end + m_prev, jnp.max(d_end_r, axis=1, keepdims=True))
        w_end_c = jnp.exp(d_end_c - m_new)
        decay = jnp.exp(f_end + m_prev - m_new)
        kw = k * w_end_c
        c_s[h] = decay * c_prev + lax.dot_general(
            kw.astype(BF16), vb, (((0,), (0,)), ((), ())), preferred_element_type=F32)
        n_s[h] = decay * n_prev + jnp.sum(kw, axis=0, keepdims=True)
        m_s[h] = m_new

        mu = jnp.mean(hh, axis=1, keepdims=True)
        hc = hh - mu
        hn = hc * lax.rsqrt(jnp.mean(hc * hc, axis=1, keepdims=True) + HEAD_EPS)
        hn = hn * nw_ref[:, vs]
        y_ref[:, vs] = (jax.nn.sigmoid(o_ref[:, vs]) * hn).astype(BF16)
        return carry

    lax.fori_loop(0, NH_A, head, 0, unroll=2)

    @pl.when(c == nc - 1)
    def _():
        ct_ref[0] = c_s[...]
        nt_ref[0] = n_s[...]
        mt_ref[0] = m_s[...]


def mlstm_scan(proj, gates, norm_w, *, row0, n_seq, n_chunk, L, init=None):
    rb0 = row0 // L
    rows = n_seq * n_chunk * L
    g_rows = lax.slice_in_dim(gates, row0, row0 + rows, axis=0)
    g_t = g_rows[:, :2 * NH_A].T.reshape(2 * NH_A, n_seq * n_chunk, 1, L)
    rmap = lambda s, c: (rb0 + s * n_chunk + c)
    in_specs = [
        pl.BlockSpec((L, QK_A), lambda s, c: (rmap(s, c), 0)),
        pl.BlockSpec((L, QK_A), lambda s, c: (rmap(s, c), 1)),
        pl.BlockSpec((L, VD_A), lambda s, c: (rmap(s, c), 1)),
        pl.BlockSpec((L, VD_A), lambda s, c: (rmap(s, c), 2)),
        pl.BlockSpec((L, LANES), lambda s, c: (rmap(s, c), 0)),
        pl.BlockSpec((2 * NH_A, 1, 1, L), lambda s, c: (0, s * n_chunk + c, 0, 0)),
        pl.BlockSpec((1, VD_A), lambda s, c: (0, 0)),
    ]
    args = [proj, proj, proj, proj, gates, g_t, norm_w.reshape(1, VD_A)]
    state_specs = [
        pl.BlockSpec((1, NH_A, DK_A, DV_A), lambda s, c: (s, 0, 0, 0)),
        pl.BlockSpec((1, NH_A, 1, DK_A), lambda s, c: (s, 0, 0, 0)),
        pl.BlockSpec((1, NH_A, 1, 1), lambda s, c: (s, 0, 0, 0)),
    ]
    if init is not None:
        c0, n0, m0, layer = init
        nl = c0.shape[0]
        in_specs += [
            pl.BlockSpec((None, 1, NH_A, DK_A, DV_A), lambda s, c: (layer, s, 0, 0, 0)),
            pl.BlockSpec((None, 1, NH_A, 1, DK_A), lambda s, c: (layer, s, 0, 0, 0)),
            pl.BlockSpec((None, 1, NH_A, 1, 1), lambda s, c: (layer, s, 0, 0, 0)),
        ]
        args += [c0, n0.reshape(nl, n_seq, NH_A, 1, DK_A), m0.reshape(nl, n_seq, NH_A, 1, 1)]
    est = 2 * (2 * _nbytes((L, QK_A), F32) + 2 * _nbytes((L, VD_A), F32) + _nbytes((L, VD_A), BF16))
    est += (3 if init is None else 5) * _nbytes((NH_A, DK_A, DV_A), F32)
    est += 8 * _nbytes((L, max(L, DV_A)), F32)
    y, ct, nt, mt = pl.pallas_call(
        functools.partial(_mlstm_kernel, L=L, has_init=init is not None),
        grid=(n_seq, n_chunk),
        in_specs=in_specs,
        out_specs=[pl.BlockSpec((L, VD_A), lambda s, c: (s * n_chunk + c, 0))] + state_specs,
        out_shape=[
            jax.ShapeDtypeStruct((rows, VD_A), BF16),
            jax.ShapeDtypeStruct((n_seq, NH_A, DK_A, DV_A), F32),
            jax.ShapeDtypeStruct((n_seq, NH_A, 1, DK_A), F32),
            jax.ShapeDtypeStruct((n_seq, NH_A, 1, 1), F32),
        ],
        scratch_shapes=[
            pltpu.VMEM((NH_A, DK_A, DV_A), F32),
            pltpu.VMEM((NH_A, 1, DK_A), F32),
            pltpu.VMEM((NH_A, 1, 1), F32),
        ],
        compiler_params=pltpu.CompilerParams(
            dimension_semantics=("arbitrary", "arbitrary"), vmem_limit_bytes=_vmem_limit(est)),
        name="mlstm_scan_init" if init is not None else "mlstm_scan",
    )(*args)
    return y, ct, nt.reshape(n_seq, NH_A, DK_A), mt.reshape(n_seq, NH_A)


def _hgrn_kernel(q_ref, f_ref, i_ref, g_ref, bl_ref, nw_ref, s0_ref, y_ref, st_ref, s_s,
                 *, layer_j, n_prompt_chunks, pair_unroll):
    L = CHUNK
    W = 2 * DK_B
    n_pairs = NH_B // 2
    g_id = pl.program_id(0)

    @pl.when(g_id == 0)
    def _():
        s_s[...] = jnp.zeros_like(s_s)

    @pl.when(g_id >= n_prompt_chunks)
    def _():
        def load(hp, carry):
            s_s[hp] = jnp.concatenate([s0_ref[0, 2 * hp].T, s0_ref[0, 2 * hp + 1].T], axis=1)
            return carry
        lax.fori_loop(0, n_pairs, load, 0)

    row = lax.broadcasted_iota(jnp.int32, (L, W), 0)
    t_idx = lax.broadcasted_iota(jnp.int32, (2 * L, 2 * L), 0)
    s_idx = lax.broadcasted_iota(jnp.int32, (2 * L, 2 * L), 1)
    levels = [1, 2, 4, 8, 16, 32]
    upper = {mm: (row & mm) != 0 for mm in levels}
    pair = {mm: ((t_idx ^ s_idx) < 2 * mm) & ((t_idx & mm) != 0) & ((s_idx & mm) == 0) for mm in levels}
    diag = t_idx == s_idx
    nt_dims = (((1,), (1,)), ((), ()))
    tn_dims = (((0,), (0,)), ((), ()))
    zero_half = jnp.zeros((L, DK_B), BF16)

    def block_diag(xw):
        top = jnp.concatenate([xw[:, :DK_B], zero_half], axis=1)
        bot = jnp.concatenate([zero_half, xw[:, DK_B:]], axis=1)
        return jnp.concatenate([top, bot], axis=0)

    def stack_heads(xw):
        return jnp.concatenate([xw[:, :DK_B], xw[:, DK_B:]], axis=0)

    def head(hp, carry):
        sl = pl.ds(pl.multiple_of(hp * W, W), W)
        q = q_ref[:, sl]
        fz = f_ref[:, sl]
        inp = i_ref[:, sl]
        gate = g_ref[:, sl]
        bl = bl_ref[:, sl]
        e = jnp.exp(bl - jnp.max(bl, axis=0, keepdims=True))
        p = e / jnp.sum(e, axis=0, keepdims=True)
        lb = jnp.zeros((1, W), F32)
        for r in range(1, layer_j + 1):
            lb = lb + p[r:r + 1]
        sig = jax.nn.sigmoid(fz)
        oml = 1.0 - lb
        logf = jnp.log(lb + oml * sig)
        k = oml * (1.0 - sig)

        ib = stack_heads(inp.astype(BF16))
        s_acc = jnp.where(
            diag, lax.dot_general(block_diag(q.astype(BF16)), block_diag(k.astype(BF16)), nt_dims,
                                  preferred_element_type=F32), 0.0)
        p_sum = logf
        r_sum = jnp.zeros_like(logf)
        g_sum = logf
        for mm in levels:
            up = upper[mm]
            sib = jnp.where(up, pltpu.roll(g_sum, mm, 0), pltpu.roll(g_sum, L - mm, 0))
            x = block_diag((jnp.where(up, q, k) * jnp.exp(jnp.where(up, p_sum, r_sum))).astype(BF16))
            s_m = lax.dot_general(x, x, nt_dims, preferred_element_type=F32)
            s_acc = jnp.where(pair[mm], s_m, s_acc)
            p_sum = p_sum + jnp.where(up, sib, 0.0)
            r_sum = r_sum + jnp.where(up, 0.0, sib)
            g_sum = g_sum + sib
        s_prev = s_s[hp]
        o = lax.dot_general(block_diag((q * jnp.exp(p_sum)).astype(BF16)), s_prev.astype(BF16), nt_dims,
                            preferred_element_type=F32)
        o = o + jnp.dot(s_acc.astype(BF16), ib, preferred_element_type=F32)
        k_end = block_diag((k * jnp.exp(r_sum)).astype(BF16))
        s_s[hp] = jnp.exp(g_sum[0:1]) * s_prev + lax.dot_general(ib, k_end, tn_dims, preferred_element_type=F32)

        o = o * lax.rsqrt(jnp.mean(o * o, axis=1, keepdims=True) + HEAD_EPS)
        post = nw_ref[:, sl] * (gate * jax.nn.sigmoid(gate))
        y_ref[:, sl] = jnp.concatenate([o[:L] * post[:, :DV_B], o[L:] * post[:, DV_B:]], axis=1).astype(BF16)
        return carry

    lax.fori_loop(0, n_pairs, head, 0, unroll=pair_unroll)

    @pl.when(g_id >= n_prompt_chunks - 1)
    def _():
        def store(hp, carry):
            st = s_s[hp]
            st_ref[0, 2 * hp] = st[:, :DK_B].T
            st_ref[0, 2 * hp + 1] = st[:, DK_B:].T
            return carry
        lax.fori_loop(0, n_pairs, store, 0)


def hgrn_scan(proj, b_lower, norm_w, s0, *, layer_j, n_prompt_chunks, pair_unroll=8):
    m = proj.shape[0]
    n_chunks = m // CHUNK
    n_seq = n_chunks - n_prompt_chunks + 1
    n_b = b_lower.shape[0]
    col = lambda cb: pl.BlockSpec((CHUNK, D_MODEL), lambda g: (g, cb))
    est = 2 * (4 * _nbytes((CHUNK, D_MODEL), F32) + _nbytes((CHUNK, D_MODEL), BF16))
    est += 5 * _nbytes((NH_B, DK_B, DV_B), F32)
    y, st = pl.pallas_call(
        functools.partial(_hgrn_kernel, layer_j=layer_j, n_prompt_chunks=n_prompt_chunks,
                          pair_unroll=pair_unroll),
        grid=(n_chunks,),
        in_specs=[col(0), col(1), col(2), col(3),
                  pl.BlockSpec((n_b, D_MODEL), lambda g: (0, 0)),
                  pl.BlockSpec((1, D_MODEL), lambda g: (0, 0)),
                  pl.BlockSpec((None, 1, NH_B, DK_B, DV_B),
                               lambda g: (layer_j, jnp.maximum(g - n_prompt_chunks, 0), 0, 0, 0))],
        out_specs=[pl.BlockSpec((CHUNK, D_MODEL), lambda g: (g, 0)),
                   pl.BlockSpec((1, NH_B, DK_B, DV_B),
                                lambda g: (jnp.maximum(g - (n_prompt_chunks - 1), 0), 0, 0, 0))],
        out_shape=[jax.ShapeDtypeStruct((m, D_MODEL), BF16),
                   jax.ShapeDtypeStruct((n_seq, NH_B, DK_B, DV_B), F32)],
        scratch_shapes=[pltpu.VMEM((NH_B // 2, DV_B, 2 * DK_B), F32)],
        compiler_params=pltpu.CompilerParams(
            dimension_semantics=("arbitrary",), vmem_limit_bytes=_vmem_limit(est)),
        name="hgrn_scan",
    )(proj, proj, proj, proj, b_lower, norm_w.reshape(1, D_MODEL), s0)
    return y, st


def _ffn_kernel(x_ref, wga_ref, wgb_ref, wua_ref, wub_ref, wdc_ref, cw_ref, cb_ref, st_ref,
                h_ref, so_ref, wdo_ref, carry_ref, wgs_ref, wus_ref,
                *, nm, nq, bt, rb, last_half_valid, wd_valid_blocks, prompt_end, sample_starts, seq_len):
    p = pl.program_id(0)
    i = pl.program_id(1)
    bm = x_ref.shape[0]
    crows = wga_ref.shape[0]

    def cast_next_pair():
        rows = pl.ds(pl.multiple_of(i * crows, crows), crows)
        for half, (g_ref, u_ref) in enumerate(((wga_ref, wua_ref), (wgb_ref, wub_ref))):
            wgs_ref[p % 2, half, rows, :] = g_ref[...].astype(BF16)
            wus_ref[p % 2, half, rows, :] = u_ref[...].astype(BF16)

    @pl.when(p == 0)
    def _():
        cast_next_pair()

    @pl.when(p > 0)
    def _():
        slot = (p + 1) % 2
        rows8 = lax.broadcasted_iota(jnp.int32, (SUBLANES, bt), 0)
        is_last = i == nm - 1
        drop = [None, None if last_half_valid else (p == nq)]
        cols = [slice(half * bt, (half + 1) * bt) for half in range(2)]
        w0 = [cw_ref[0:1, c] for c in cols]
        w1 = [cw_ref[1:2, c] for c in cols]
        w2 = [cw_ref[2:3, c] for c in cols]
        cb = [cb_ref[:, c] for c in cols]

        def act(half, a_cur, a_p1, a_p2, u_cur):
            c = cb[half] + a_p2 * w0[half]
            c = c + a_p1 * w1[half]
            c = c + a_cur * w2[half]
            out = c * jax.nn.sigmoid(c) * u_cur
            if drop[half] is not None:
                out = jnp.where(drop[half], 0.0, out)
            return out.astype(BF16)

        prev8 = [jnp.where(i == 0, 0.0, carry_ref[:, c]) for c in cols]
        for r0 in range(0, bm, rb):
            xs = x_ref[r0:r0 + rb, :]
            for half in range(2):
                c = cols[half]
                a = jnp.dot(xs, wgs_ref[slot, half], preferred_element_type=F32)
                u = jnp.dot(xs, wus_ref[slot, half], preferred_element_type=F32)
                h_ref[r0:r0 + rb, c] = act(half, a, pltpu.roll(a, 1, 0), pltpu.roll(a, 2, 0), u)

                def patch(off, before8):
                    blk = a[off:off + SUBLANES]
                    p1 = jnp.where(rows8 < 1, pltpu.roll(before8, 1, 0), pltpu.roll(blk, 1, 0))
                    p2 = jnp.where(rows8 < 2, pltpu.roll(before8, 2, 0), pltpu.roll(blk, 2, 0))
                    h_ref[r0 + off:r0 + off + SUBLANES, c] = act(half, blk, p1, p2, u[off:off + SUBLANES])

                patch(0, prev8[half])
                for s, start in enumerate(sample_starts):
                    off = start - r0
                    if SUBLANES <= off < rb:
                        patch(off, jnp.where(is_last, st_ref[s, :, c], a[off - SUBLANES:off]))
                    end = off + seq_len
                    if SUBLANES <= end <= rb:
                        so_ref[1 + s, :, c] = a[end - SUBLANES:end]
                if SUBLANES <= prompt_end - r0 <= rb:
                    so_ref[0, :, c] = a[prompt_end - r0 - SUBLANES:prompt_end - r0]
                prev8[half] = a[rb - SUBLANES:rb]
        for half in range(2):
            carry_ref[:, cols[half]] = prev8[half]

        cast_next_pair()
        wd_rows_valid = (p - 1) * nm + i < wd_valid_blocks
        wdo_ref[...] = jnp.where(wd_rows_valid, wdc_ref[...], 0.0).astype(BF16)


def ffn_gate_up(xb, wg, wu, wd, conv_w, conv_b, state8, layer, *, bm, bt, rb, n_prompt_rows, seq_len):
    m, d = xb.shape
    n = wg.shape[2]
    nm, n_tiles = m // bm, n // bt
    nq = -(-n_tiles // 2)
    n_pad = nq * 2 * bt
    crows = d // nm
    drows = 2 * bt // nm
    n_seq = state8.shape[0]
    tile0 = (nm - 1) * bm
    prompt_end = n_prompt_rows - tile0
    sample_starts = tuple(prompt_end + s * seq_len for s in range(n_seq))
    assert m % bm == 0 and n % bt == 0 and bm % rb == 0 and rb % SUBLANES == 0
    assert d % nm == 0 and crows % SUBLANES == 0 and (2 * bt) % nm == 0 and drows % SUBLANES == 0
    assert n % drows == 0 and conv_w.shape[2] == n_pad and conv_b.shape[2] == n_pad and state8.shape[2] == n_pad
    assert m == n_prompt_rows + n_seq * seq_len and SUBLANES <= prompt_end
    assert all(st % SUBLANES == 0 and st % rb != 0 for st in sample_starts)
    est = 2 * (_nbytes((bm, d), BF16) + 4 * _nbytes((crows, bt), F32) + _nbytes((bm, 2 * bt), BF16))
    est += 2 * (_nbytes((drows, d), F32) + _nbytes((drows, d), BF16))
    est += 8 * _nbytes((d, bt), BF16)
    est += 6 * _nbytes((rb, bt), F32) + 4 * _nbytes((n_seq + 1, SUBLANES, 2 * bt), F32)
    row = lambda p, i: jnp.where(p > 0, i, 0)
    jq = lambda p: jnp.maximum(p - 1, 0)
    chunk = lambda p, i: jnp.where(p < nq, i, nm - 1)
    tile_a = lambda p: 2 * jnp.minimum(p, nq - 1)
    tile_b = lambda p: jnp.minimum(tile_a(p) + 1, n_tiles - 1)
    wspec_a = pl.BlockSpec((None, crows, bt), lambda p, i: (layer, chunk(p, i), tile_a(p)))
    wspec_b = pl.BlockSpec((None, crows, bt), lambda p, i: (layer, chunk(p, i), tile_b(p)))
    dblock = lambda p, i: jq(p) * nm + row(p, i)
    wd_valid_blocks = n // drows
    pair = lambda p, i: (layer, 0, jq(p))
    return pl.pallas_call(
        functools.partial(_ffn_kernel, nm=nm, nq=nq, bt=bt, rb=rb, last_half_valid=n_tiles % 2 == 0,
                          wd_valid_blocks=wd_valid_blocks, prompt_end=prompt_end,
                          sample_starts=sample_starts, seq_len=seq_len),
        grid=(nq + 1, nm),
        in_specs=[pl.BlockSpec((bm, d), lambda p, i: (row(p, i), 0)),
                  wspec_a, wspec_b, wspec_a, wspec_b,
                  pl.BlockSpec((None, drows, d),
                               lambda p, i: (layer, jnp.minimum(dblock(p, i), wd_valid_blocks - 1), 0)),
                  pl.BlockSpec((None, CONV_W, 2 * bt), pair),
                  pl.BlockSpec((None, 1, 2 * bt), pair),
                  pl.BlockSpec((n_seq, SUBLANES, 2 * bt), lambda p, i: (0, 0, jq(p)))],
        out_specs=[pl.BlockSpec((bm, 2 * bt), lambda p, i: (row(p, i), jq(p))),
                   pl.BlockSpec((n_seq + 1, SUBLANES, 2 * bt), lambda p, i: (0, 0, jq(p))),
                   pl.BlockSpec((drows, d), lambda p, i: (dblock(p, i), 0))],
        out_shape=[jax.ShapeDtypeStruct((m, n_pad), BF16),
                   jax.ShapeDtypeStruct((n_seq + 1, SUBLANES, n_pad), F32),
                   jax.ShapeDtypeStruct((n_pad, d), BF16)],
        scratch_shapes=[pltpu.VMEM((SUBLANES, 2 * bt), F32),
                        pltpu.VMEM((2, 2, d, bt), BF16),
                        pltpu.VMEM((2, 2, d, bt), BF16)],
        compiler_params=pltpu.CompilerParams(
            dimension_semantics=("arbitrary", "arbitrary"), vmem_limit_bytes=_vmem_limit(est)),
        name="ffn_gate_up",
    )(xb, wg, wg, wu, wu, wd, conv_w, conv_b, state8)


def kernel(x_prompt, x_sample, state_mlstm_C, state_mlstm_n, state_mlstm_m, state_hgrn_S, state_ffn_conv,
           ln_g, ln_b, w_a_in, b_a_gate, a_norm_w, w_a_out, w_b_in, b_lower, b_norm_w, w_b_out,
           w_ffn_gate, w_ffn_up, w_ffn_down, ffn_conv_w, ffn_conv_b):
    bp, tp, d = x_prompt.shape
    bs, ts, _ = x_sample.shape
    mp = bp * tp
    ms = bs * ts
    m = mp + ms
    assert bp == 1 and ts == CHUNK and d == D_MODEL
    l_prompt = 128

    x = jnp.concatenate([x_prompt.reshape(mp, d), x_sample.reshape(ms, d)], axis=0)
    xb = x.astype(BF16)
    w_a_in_t = jnp.swapaxes(w_a_in, 1, 2)
    ff_pad = -D_FF % (2 * FFN_BT)
    conv_w_p = jnp.pad(ffn_conv_w, ((0, 0), (0, 0), (0, ff_pad)))
    conv_b_p = jnp.pad(ffn_conv_b, ((0, 0), (0, ff_pad))).reshape(DEPTH, 1, D_FF + ff_pad)

    p_c, p_n, p_m, p_s, p_conv = [], [], [], [], []
    s_c, s_n, s_m, s_s, s_conv = [], [], [], [], []
    for l in range(DEPTH):
        j = l // 2
        if l % 2 == 0:
            w_gate = jnp.pad(w_a_in_t[j, A_MAIN:, :], ((0, LANES - 2 * NH_A), (0, 0)))
            b_gate = jnp.pad(b_a_gate[j], (0, LANES - 2 * NH_A)).reshape(1, LANES)
            proj = matmul_wcast(xb, w_a_in_t, j, n_cols=A_MAIN, w_t=True, bm=1088, bn=1024,
                                name="mlstm_in_proj")
            gates = mlstm_gates(xb, w_gate, b_gate)
            y_p, c_p, n_p, m_p = mlstm_scan(proj, gates, a_norm_w[j], row0=0, n_seq=1,
                                            n_chunk=mp // l_prompt, L=l_prompt)
            y_s, c_s2, n_s2, m_s2 = mlstm_scan(
                proj, gates, a_norm_w[j], row0=mp, n_seq=bs, n_chunk=1, L=ts,
                init=(state_mlstm_C, state_mlstm_n, state_mlstm_m, j))
            p_c.append(c_p); p_n.append(n_p); p_m.append(m_p)
            s_c.append(c_s2); s_n.append(n_s2); s_m.append(m_s2)
            y_mix = jnp.concatenate([y_p, y_s], axis=0)
            w_out = w_a_out
        else:
            proj = matmul_wcast(xb, w_b_in, j, bm=1088, bn=1024, name="hgrn_in_proj")
            y_mix, st = hgrn_scan(proj, b_lower, b_norm_w[j], state_hgrn_S, layer_j=j,
                                  n_prompt_chunks=mp // CHUNK)
            p_s.append(st[:1]); s_s.append(st[1:])
            w_out = w_b_out
        y = matmul_wcast(y_mix, w_out, j, bm=544, bn=1024, res=x, alpha=ALPHA, name="mixer_out_proj")
        x, xb = layer_norm(y, ln_g[l, 0], ln_b[l, 0])

        st8 = jnp.pad(state_ffn_conv[l], ((0, 0), (SUBLANES - (CONV_W - 1), 0), (0, ff_pad)))
        hmid, so, wd_b = ffn_gate_up(xb, w_ffn_gate, w_ffn_up, w_ffn_down, conv_w_p, conv_b_p, st8, l,
                                     bm=1088, bt=FFN_BT, rb=272, n_prompt_rows=mp, seq_len=ts)
        conv_new = so[:, SUBLANES - (CONV_W - 1):, :D_FF]
        p_conv.append(conv_new[:1]); s_conv.append(conv_new[1:])
        y = matmul_xres(hmid, wd_b, x, bm=544, bn=512, alpha=ALPHA, name="ffn_down_proj")
        if l < DEPTH - 1:
            x, xb = layer_norm(y, ln_g[l, 1], ln_b[l, 1])

    (y_prompt,) = layer_norm(y, ln_g[DEPTH - 1, 1], ln_b[DEPTH - 1, 1], rows=mp, emit_bf16=False)
    (y_sample,) = layer_norm(y, ln_g[DEPTH - 1, 1], ln_b[DEPTH - 1, 1], row0=mp, emit_bf16=False)
    y_prompt = y_prompt.reshape(bp, tp, d)
    y_sample = y_sample.reshape(bs, ts, d)
    return (y_prompt, y_sample,
            jnp.stack(p_c), jnp.stack(p_n), jnp.stack(p_m), jnp.stack(p_s), jnp.stack(p_conv),
            jnp.stack(s_c), jnp.stack(s_n), jnp.stack(s_m), jnp.stack(s_s), jnp.stack(s_conv))
```

```python
import functools

import jax
import jax.numpy as jnp
from jax import lax
from jax.experimental import pallas as pl
from jax.experimental.pallas import tpu as pltpu

F32 = jnp.float32
BF16 = jnp.bfloat16

D_MODEL = 4096
DEPTH = 4
CHUNK = 64
NH_A = 8
DK_A = 256
DV_A = 512
QK_A = NH_A * DK_A
VD_A = NH_A * DV_A
A_MAIN = 2 * QK_A + 2 * VD_A
GATE_SOFTCAP = 15.0
NH_B = 32
DK_B = 128
DV_B = 128
D_FF = 11008
FFN_BT = 256
CONV_W = 3
ALPHA = (2 * DEPTH) ** 0.25
LN_EPS = 1e-5
HEAD_EPS = 1e-6

LANES = 128
SUBLANES = 8
VMEM_LIMIT_CAP = 58 * 1024 * 1024


def _vmem_limit(nbytes):
    return int(min(VMEM_LIMIT_CAP, nbytes * 1.2 + (6 << 20)))


def _nbytes(shape, dtype):
    n = 1
    for s in shape:
        n *= s
    return n * jnp.dtype(dtype).itemsize


def _finish(acc, r_ref, o_ref, alpha):
    if r_ref is not None:
        acc = alpha * r_ref[...] + acc
    o_ref[...] = acc.astype(o_ref.dtype)


def _mm_wcast_kernel(*refs, nn, nm, w_t, has_res, alpha):
    if has_res:
        x_ref, wc_ref, r_ref, o_ref, wb_ref = refs
    else:
        x_ref, wc_ref, o_ref, wb_ref = refs
        r_ref = None
    p = pl.program_id(0)
    i = pl.program_id(1)
    crows = wc_ref.shape[0]

    @pl.when(p < nn)
    def _():
        wb_ref[p % 2, pl.ds(pl.multiple_of(i * crows, crows), crows), :] = wc_ref[...].astype(BF16)

    @pl.when(p > 0)
    def _():
        wb = wb_ref[(p + 1) % 2]
        if w_t:
            acc = lax.dot_general(x_ref[...], wb, (((1,), (1,)), ((), ())), preferred_element_type=F32)
        else:
            acc = jnp.dot(x_ref[...], wb, preferred_element_type=F32)
        _finish(acc, r_ref, o_ref, alpha)


def matmul_wcast(x, w, layer, *, bm, bn, n_cols=None, w_t=False, res=None, alpha=1.0, out_dtype=F32,
                 name="mm"):
    m, kdim = x.shape
    n = w.shape[1 if w_t else 2] if n_cols is None else n_cols
    nm, nn = m // bm, n // bn
    crows = (bn if w_t else kdim) // nm
    assert m % bm == 0 and n % bn == 0 and (bn if w_t else kdim) % nm == 0 and crows % SUBLANES == 0
    row = lambda p, i: jnp.where(p > 0, i, 0)
    xi = lambda p, i: (row(p, i), 0)
    oi = lambda p, i: (row(p, i), jnp.maximum(p - 1, 0))
    chunk = lambda p, i: jnp.where(p < nn, i, nm - 1)
    tile = lambda p: jnp.minimum(p, nn - 1)
    if w_t:
        wspec = pl.BlockSpec((None, crows, kdim), lambda p, i: (layer, tile(p) * nm + chunk(p, i), 0))
        wb_shape = (2, bn, kdim)
    else:
        wspec = pl.BlockSpec((None, crows, bn), lambda p, i: (layer, chunk(p, i), tile(p)))
        wb_shape = (2, kdim, bn)
    in_specs = [pl.BlockSpec((bm, kdim), xi), wspec]
    args = [x, w]
    est = 2 * (_nbytes((bm, kdim), BF16) + _nbytes(wspec.block_shape[1:], F32) + _nbytes((bm, bn), out_dtype))
    est += _nbytes((bm, bn), F32) + _nbytes(wb_shape, BF16)
    if res is not None:
        in_specs.append(pl.BlockSpec((bm, bn), oi))
        args.append(res)
        est += 2 * _nbytes((bm, bn), res.dtype)
    return pl.pallas_call(
        functools.partial(_mm_wcast_kernel, nn=nn, nm=nm, w_t=w_t, has_res=res is not None, alpha=alpha),
        grid=(nn + 1, nm),
        in_specs=in_specs,
        out_specs=pl.BlockSpec((bm, bn), oi),
        out_shape=jax.ShapeDtypeStruct((m, n), out_dtype),
        scratch_shapes=[pltpu.VMEM(wb_shape, BF16)],
        compiler_params=pltpu.CompilerParams(
            dimension_semantics=("arbitrary", "arbitrary"), vmem_limit_bytes=_vmem_limit(est)),
        name=name,
    )(*args)


def _mm_kernel(x_ref, w_ref, r_ref, o_ref, *, alpha):
    _finish(jnp.dot(x_ref[...], w_ref[...], preferred_element_type=F32), r_ref, o_ref, alpha)


def matmul_xres(x, w, res, *, bm, bn, alpha, name="mm"):
    m, kdim = x.shape
    n = w.shape[1]
    assert m % bm == 0 and n % bn == 0
    oi = lambda i, j: (i, j)
    est = 2 * (_nbytes((bm, kdim), BF16) + _nbytes((kdim, bn), BF16) + 2 * _nbytes((bm, bn), F32))
    est += _nbytes((bm, bn), F32)
    return pl.pallas_call(
        functools.partial(_mm_kernel, alpha=alpha),
        grid=(m // bm, n // bn),
        in_specs=[pl.BlockSpec((bm, kdim), lambda i, j: (i, 0)),
                  pl.BlockSpec((kdim, bn), lambda i, j: (0, j)),
                  pl.BlockSpec((bm, bn), oi)],
        out_specs=pl.BlockSpec((bm, bn), oi),
        out_shape=jax.ShapeDtypeStruct((m, n), F32),
        compiler_params=pltpu.CompilerParams(
            dimension_semantics=("arbitrary", "arbitrary"), vmem_limit_bytes=_vmem_limit(est)),
        name=name,
    )(x, w, res)


def _ln_kernel(y_ref, g_ref, b_ref, o_ref, ob_ref=None):
    y = y_ref[...]
    mu = jnp.mean(y, axis=-1, keepdims=True)
    yc = y - mu
    var = jnp.mean(yc * yc, axis=-1, keepdims=True)
    out = yc * lax.rsqrt(var + LN_EPS) * g_ref[...] + b_ref[...]
    o_ref[...] = out
    if ob_ref is not None:
        ob_ref[...] = out.astype(BF16)


def layer_norm(y, g, b, *, bm=256, row0=0, rows=None, emit_bf16=True):
    m, d = y.shape
    rows = m - row0 if rows is None else rows
    assert row0 % bm == 0 and rows % bm == 0
    blk0 = row0 // bm
    row = pl.BlockSpec((bm, d), lambda i: (i, 0))
    vec = pl.BlockSpec((1, d), lambda i: (0, 0))
    est = 2 * (2 * _nbytes((bm, d), F32) + _nbytes((bm, d), BF16)) + 2 * _nbytes((bm, d), F32)
    out_specs = [row, row] if emit_bf16 else [row]
    out_shape = [jax.ShapeDtypeStruct((rows, d), F32)]
    if emit_bf16:
        out_shape.append(jax.ShapeDtypeStruct((rows, d), BF16))
    return pl.pallas_call(
        _ln_kernel,
        grid=(rows // bm,),
        in_specs=[pl.BlockSpec((bm, d), lambda i: (blk0 + i, 0)), vec, vec],
        out_specs=out_specs,
        out_shape=out_shape,
        compiler_params=pltpu.CompilerParams(
            dimension_semantics=("arbitrary",), vmem_limit_bytes=_vmem_limit(est)),
        name="layer_norm",
    )(y, g.reshape(1, d), b.reshape(1, d))


def _log_sigmoid(z):
    return jnp.minimum(z, 0.0) - jnp.log1p(jnp.exp(-jnp.abs(z)))


def _gates_kernel(x_ref, w_ref, b_ref, o_ref):
    g = lax.dot_general(x_ref[...], w_ref[...].astype(BF16), (((1,), (1,)), ((), ())),
                        preferred_element_type=F32)
    g = g + b_ref[...]
    sc = GATE_SOFTCAP * jnp.tanh(g / GATE_SOFTCAP)
    lane = lax.broadcasted_iota(jnp.int32, sc.shape, 1)
    o_ref[...] = jnp.where(lane < NH_A, sc, _log_sigmoid(sc))


def mlstm_gates(xb, w_gate_t, b_gate, *, bm=1088):
    m, d = xb.shape
    est = 2 * (_nbytes((bm, d), BF16) + 2 * _nbytes((LANES, d), F32) + _nbytes((bm, LANES), F32))
    return pl.pallas_call(
        _gates_kernel,
        grid=(m // bm,),
        in_specs=[pl.BlockSpec((bm, d), lambda i: (i, 0)),
                  pl.BlockSpec((LANES, d), lambda i: (0, 0)),
                  pl.BlockSpec((1, LANES), lambda i: (0, 0))],
        out_specs=pl.BlockSpec((bm, LANES), lambda i: (i, 0)),
        out_shape=jax.ShapeDtypeStruct((m, LANES), F32),
        compiler_params=pltpu.CompilerParams(
            dimension_semantics=("arbitrary",), vmem_limit_bytes=_vmem_limit(est)),
        name="mlstm_gates",
    )(xb, w_gate_t, b_gate)


def _mlstm_kernel(*refs, L, has_init):
    if has_init:
        (q_ref, k_ref, v_ref, o_ref, gc_ref, gr_ref, nw_ref, c0_ref, n0_ref, m0_ref,
         y_ref, ct_ref, nt_ref, mt_ref, c_s, n_s, m_s) = refs
    else:
        (q_ref, k_ref, v_ref, o_ref, gc_ref, gr_ref, nw_ref,
         y_ref, ct_ref, nt_ref, mt_ref, c_s, n_s, m_s) = refs
    c = pl.program_id(1)
    nc = pl.num_programs(1)

    @pl.when(c == 0)
    def _():
        if has_init:
            c_s[...] = c0_ref[0]
            n_s[...] = n0_ref[0]
            m_s[...] = m0_ref[0]
        else:
            c_s[...] = jnp.zeros_like(c_s)
            n_s[...] = jnp.zeros_like(n_s)
            m_s[...] = jnp.zeros_like(m_s)

    t_idx = lax.broadcasted_iota(jnp.int32, (L, L), 0)
    s_idx = lax.broadcasted_iota(jnp.int32, (L, L), 1)
    causal = s_idx <= t_idx
    lane = lax.broadcasted_iota(jnp.int32, (L, LANES), 1)
    gcol = gc_ref[...]

    def head(h, carry):
        qs = pl.ds(pl.multiple_of(h * DK_A, DK_A), DK_A)
        vs = pl.ds(pl.multiple_of(h * DV_A, DV_A), DV_A)
        q = q_ref[:, qs]
        k = k_ref[:, qs] * (DK_A ** -0.5)
        v = v_ref[:, vs]
        qb = q.astype(BF16)
        kb = k.astype(BF16)
        vb = v.astype(BF16)
        ig_r = gr_ref[h, 0]
        lf_r = gr_ref[NH_A + h, 0]
        ig_c = jnp.sum(jnp.where(lane == h, gcol, 0.0), axis=1, keepdims=True)
        lf_c = jnp.sum(jnp.where(lane == NH_A + h, gcol, 0.0), axis=1, keepdims=True)
        f_c = jnp.sum(jnp.where(causal, lf_r, 0.0), axis=1, keepdims=True)
        f_r = jnp.sum(jnp.where(t_idx <= s_idx, lf_c, 0.0), axis=0, keepdims=True)
        f_end = jnp.sum(lf_r, axis=1, keepdims=True)
        m_prev = m_s[h]
        c_prev = c_s[h]
        n_prev = n_s[h]

        d = jnp.where(causal, f_c - f_r + ig_r, -jnp.inf)
        g = f_c + m_prev
        m_t = jnp.maximum(g, jnp.max(d, axis=1, keepdims=True))
        w = jnp.exp(d - m_t)
        inter = jnp.exp(g - m_t)
        s = lax.dot_general(qb, kb, (((1,), (1,)), ((), ())), preferred_element_type=F32) * w
        num = jnp.dot(s.astype(BF16), vb, preferred_element_type=F32)
        num = num + inter * jnp.dot(qb, c_prev.astype(BF16), preferred_element_type=F32)
        den = jnp.sum(s, axis=1, keepdims=True) + inter * jnp.sum(q * n_prev, axis=1, keepdims=True)
        hh = num / jnp.maximum(jnp.abs(den), jnp.exp(-m_t))

        d_end_r = f_end - f_r + ig_r
        d_end_c = f_end - f_c + ig_c
        m_new = jnp.maximum(f_end + m_prev, jnp.max(d_end_r, axis=1, keepdims=True))
        w_end_c = jnp.exp(d_end_c - m_new)
        decay = jnp.exp(f_end + m_prev - m_new)
        kw = k * w_end_c
        c_s[h] = decay * c_prev + lax.dot_general(
            kw.astype(BF16), vb, (((0,), (0,)), ((), ())), preferred_element_type=F32)
        n_s[h] = decay * n_prev + jnp.sum(kw, axis=0, keepdims=True)
        m_s[h] = m_new

        mu = jnp.mean(hh, axis=1, keepdims=True)
        hc = hh - mu
        hn = hc * lax.rsqrt(jnp.mean(hc * hc, axis=1, keepdims=True) + HEAD_EPS)
        hn = hn * nw_ref[:, vs]
        y_ref[:, vs] = (jax.nn.sigmoid(o_ref[:, vs]) * hn).astype(BF16)
        return carry

    lax.fori_loop(0, NH_A, head, 0, unroll=2)

    @pl.when(c == nc - 1)
    def _():
        ct_ref[0] = c_s[...]
        nt_ref[0] = n_s[...]
        mt_ref[0] = m_s[...]


def mlstm_scan(proj, gates, norm_w, *, row0, n_seq, n_chunk, L, init=None):
    rb0 = row0 // L
    rows = n_seq * n_chunk * L
    g_rows = lax.slice_in_dim(gates, row0, row0 + rows, axis=0)
    g_t = g_rows[:, :2 * NH_A].T.reshape(2 * NH_A, n_seq * n_chunk, 1, L)
    rmap = lambda s, c: (rb0 + s * n_chunk + c)
    in_specs = [
        pl.BlockSpec((L, QK_A), lambda s, c: (rmap(s, c), 0)),
        pl.BlockSpec((L, QK_A), lambda s, c: (rmap(s, c), 1)),
        pl.BlockSpec((L, VD_A), lambda s, c: (rmap(s, c), 1)),
        pl.BlockSpec((L, VD_A), lambda s, c: (rmap(s, c), 2)),
        pl.BlockSpec((L, LANES), lambda s, c: (rmap(s, c), 0)),
        pl.BlockSpec((2 * NH_A, 1, 1, L), lambda s, c: (0, s * n_chunk + c, 0, 0)),
        pl.BlockSpec((1, VD_A), lambda s, c: (0, 0)),
    ]
    args = [proj, proj, proj, proj, gates, g_t, norm_w.reshape(1, VD_A)]
    state_specs = [
        pl.BlockSpec((1, NH_A, DK_A, DV_A), lambda s, c: (s, 0, 0, 0)),
        pl.BlockSpec((1, NH_A, 1, DK_A), lambda s, c: (s, 0, 0, 0)),
        pl.BlockSpec((1, NH_A, 1, 1), lambda s, c: (s, 0, 0, 0)),
    ]
    if init is not None:
        c0, n0, m0, layer = init
        nl = c0.shape[0]
        in_specs += [
            pl.BlockSpec((None, 1, NH_A, DK_A, DV_A), lambda s, c: (layer, s, 0, 0, 0)),
            pl.BlockSpec((None, 1, NH_A, 1, DK_A), lambda s, c: (layer, s, 0, 0, 0)),
            pl.BlockSpec((None, 1, NH_A, 1, 1), lambda s, c: (layer, s, 0, 0, 0)),
        ]
        args += [c0, n0.reshape(nl, n_seq, NH_A, 1, DK_A), m0.reshape(nl, n_seq, NH_A, 1, 1)]
    est = 2 * (2 * _nbytes((L, QK_A), F32) + 2 * _nbytes((L, VD_A), F32) + _nbytes((L, VD_A), BF16))
    est += (3 if init is None else 5) * _nbytes((NH_A, DK_A, DV_A), F32)
    est += 8 * _nbytes((L, max(L, DV_A)), F32)
    y, ct, nt, mt = pl.pallas_call(
        functools.partial(_mlstm_kernel, L=L, has_init=init is not None),
        grid=(n_seq, n_chunk),
        in_specs=in_specs,
        out_specs=[pl.BlockSpec((L, VD_A), lambda s, c: (s * n_chunk + c, 0))] + state_specs,
        out_shape=[
            jax.ShapeDtypeStruct((rows, VD_A), BF16),
            jax.ShapeDtypeStruct((n_seq, NH_A, DK_A, DV_A), F32),
            jax.ShapeDtypeStruct((n_seq, NH_A, 1, DK_A), F32),
            jax.ShapeDtypeStruct((n_seq, NH_A, 1, 1), F32),
        ],
        scratch_shapes=[
            pltpu.VMEM((NH_A, DK_A, DV_A), F32),
            pltpu.VMEM((NH_A, 1, DK_A), F32),
            pltpu.VMEM((NH_A, 1, 1), F32),
        ],
        compiler_params=pltpu.CompilerParams(
            dimension_semantics=("arbitrary", "arbitrary"), vmem_limit_bytes=_vmem_limit(est)),
        name="mlstm_scan_init" if init is not None else "mlstm_scan",
    )(*args)
    return y, ct, nt.reshape(n_seq, NH_A, DK_A), mt.reshape(n_seq, NH_A)


def _hgrn_kernel(q_ref, f_ref, i_ref, g_ref, bl_ref, nw_ref, s0_ref, y_ref, st_ref, s_s,
                 *, layer_j, n_prompt_chunks, pair_unroll):
    L = CHUNK
    W = 2 * DK_B
    n_pairs = NH_B // 2
    g_id = pl.program_id(0)

    @pl.when(g_id == 0)
    def _():
        s_s[...] = jnp.zeros_like(s_s)

    @pl.when(g_id >= n_prompt_chunks)
    def _():
        def load(hp, carry):
            s_s[hp] = jnp.concatenate([s0_ref[0, 2 * hp].T, s0_ref[0, 2 * hp + 1].T], axis=1)
            return carry
        lax.fori_loop(0, n_pairs, load, 0)

    row8 = lax.broadcasted_iota(jnp.int32, (SUBLANES, W), 0)
    t_idx = lax.broadcasted_iota(jnp.int32, (2 * L, 2 * L), 0)
    s_idx = lax.broadcasted_iota(jnp.int32, (2 * L, 2 * L), 1)
    levels = [1, 2, 4, 8, 16, 32]
    n_grp = L // SUBLANES
    groups = lambda arr: [arr[b * SUBLANES:(b + 1) * SUBLANES] for b in range(n_grp)]
    upper8 = {mm: (row8 & mm) != 0 for mm in levels if mm < SUBLANES}
    pair = {mm: ((t_idx ^ s_idx) < 2 * mm) & ((t_idx & mm) != 0) & ((s_idx & mm) == 0) for mm in levels}
    diag = t_idx == s_idx
    nt_dims = (((1,), (1,)), ((), ()))
    tn_dims = (((0,), (0,)), ((), ()))
    zero_half = jnp.zeros((L, DK_B), BF16)

    def block_diag(xw):
        top = jnp.concatenate([xw[:, :DK_B], zero_half], axis=1)
        bot = jnp.concatenate([zero_half, xw[:, DK_B:]], axis=1)
        return jnp.concatenate([top, bot], axis=0)

    def stack_heads(xw):
        return jnp.concatenate([xw[:, :DK_B], xw[:, DK_B:]], axis=0)

    def head(hp, carry):
        sl = pl.ds(pl.multiple_of(hp * W, W), W)
        q = q_ref[:, sl]
        fz = f_ref[:, sl]
        inp = i_ref[:, sl]
        gate = g_ref[:, sl]
        bl = bl_ref[:, sl]
        e = jnp.exp(bl - jnp.max(bl, axis=0, keepdims=True))
        p = e / jnp.sum(e, axis=0, keepdims=True)
        lb = jnp.zeros((1, W), F32)
        for r in range(1, layer_j + 1):
            lb = lb + p[r:r + 1]
        sig = jax.nn.sigmoid(fz)
        oml = 1.0 - lb
        logf = jnp.log(lb + oml * sig)
        k = oml * (1.0 - sig)

        ib = stack_heads(inp.astype(BF16))
        s_acc = jnp.where(
            diag, lax.dot_general(block_diag(q.astype(BF16)), block_diag(k.astype(BF16)), nt_dims,
                                  preferred_element_type=F32), 0.0)
        q8, k8 = groups(q), groups(k)
        p_sum = groups(logf)
        r_sum = [jnp.zeros((SUBLANES, W), F32)] * n_grp
        g_sum = list(p_sum)
        for mm in levels:
            xs, g_new = [], []
            for b in range(n_grp):
                if mm < SUBLANES:
                    up = upper8[mm]
                    sib = jnp.where(up, pltpu.roll(g_sum[b], mm, 0), pltpu.roll(g_sum[b], SUBLANES - mm, 0))
                    xs.append(jnp.where(up, q8[b], k8[b]) * jnp.exp(jnp.where(up, p_sum[b], r_sum[b])))
                    p_sum[b] = p_sum[b] + jnp.where(up, sib, 0.0)
                    r_sum[b] = r_sum[b] + jnp.where(up, 0.0, sib)
                else:
                    bit = mm // SUBLANES
                    sib = g_sum[b ^ bit]
                    if b & bit:
                        xs.append(q8[b] * jnp.exp(p_sum[b]))
                        p_sum[b] = p_sum[b] + sib
                    else:
                        xs.append(k8[b] * jnp.exp(r_sum[b]))
                        r_sum[b] = r_sum[b] + sib
                g_new.append(g_sum[b] + sib)
            g_sum = g_new
            x = block_diag(jnp.concatenate(xs, axis=0).astype(BF16))
            s_m = lax.dot_general(x, x, nt_dims, preferred_element_type=F32)
            s_acc = jnp.where(pair[mm], s_m, s_acc)
        p_sum = jnp.concatenate(p_sum, axis=0)
        r_sum = jnp.concatenate(r_sum, axis=0)
        g_sum = g_sum[0]
        s_prev = s_s[hp]
        o = lax.dot_general(block_diag((q * jnp.exp(p_sum)).astype(BF16)), s_prev.astype(BF16), nt_dims,
                            preferred_element_type=F32)
        o = o + jnp.dot(s_acc.astype(BF16), ib, preferred_element_type=F32)
        k_end = block_diag((k * jnp.exp(r_sum)).astype(BF16))
        s_s[hp] = jnp.exp(g_sum[0:1]) * s_prev + lax.dot_general(ib, k_end, tn_dims, preferred_element_type=F32)

        o = o * lax.rsqrt(jnp.mean(o * o, axis=1, keepdims=True) + HEAD_EPS)
        post = nw_ref[:, sl] * (gate * jax.nn.sigmoid(gate))
        y_ref[:, sl] = jnp.concatenate([o[:L] * post[:, :DV_B], o[L:] * post[:, DV_B:]], axis=1).astype(BF16)
        return carry

    lax.fori_loop(0, n_pairs, head, 0, unroll=pair_unroll)

    @pl.when(g_id >= n_prompt_chunks - 1)
    def _():
        def store(hp, carry):
            st = s_s[hp]
            st_ref[0, 2 * hp] = st[:, :DK_B].T
            st_ref[0, 2 * hp + 1] = st[:, DK_B:].T
            return carry
        lax.fori_loop(0, n_pairs, store, 0)


def hgrn_scan(proj, b_lower, norm_w, s0, *, layer_j, n_prompt_chunks, pair_unroll=8):
    m = proj.shape[0]
    n_chunks = m // CHUNK
    n_seq = n_chunks - n_prompt_chunks + 1
    n_b = b_lower.shape[0]
    col = lambda cb: pl.BlockSpec((CHUNK, D_MODEL), lambda g: (g, cb))
    est = 2 * (4 * _nbytes((CHUNK, D_MODEL), F32) + _nbytes((CHUNK, D_MODEL), BF16))
    est += 5 * _nbytes((NH_B, DK_B, DV_B), F32)
    y, st = pl.pallas_call(
        functools.partial(_hgrn_kernel, layer_j=layer_j, n_prompt_chunks=n_prompt_chunks,
                          pair_unroll=pair_unroll),
        grid=(n_chunks,),
        in_specs=[col(0), col(1), col(2), col(3),
                  pl.BlockSpec((n_b, D_MODEL), lambda g: (0, 0)),
                  pl.BlockSpec((1, D_MODEL), lambda g: (0, 0)),
                  pl.BlockSpec((None, 1, NH_B, DK_B, DV_B),
                               lambda g: (layer_j, jnp.maximum(g - n_prompt_chunks, 0), 0, 0, 0))],
        out_specs=[pl.BlockSpec((CHUNK, D_MODEL), lambda g: (g, 0)),
                   pl.BlockSpec((1, NH_B, DK_B, DV_B),
                                lambda g: (jnp.maximum(g - (n_prompt_chunks - 1), 0), 0, 0, 0))],
        out_shape=[jax.ShapeDtypeStruct((m, D_MODEL), BF16),
                   jax.ShapeDtypeStruct((n_seq, NH_B, DK_B, DV_B), F32)],
        scratch_shapes=[pltpu.VMEM((NH_B // 2, DV_B, 2 * DK_B), F32)],
        compiler_params=pltpu.CompilerParams(
            dimension_semantics=("arbitrary",), vmem_limit_bytes=_vmem_limit(est)),
        name="hgrn_scan",
    )(proj, proj, proj, proj, b_lower, norm_w.reshape(1, D_MODEL), s0)
    return y, st


def _ffn_kernel(x_ref, wga_ref, wgb_ref, wua_ref, wub_ref, wdc_ref, cw_ref, cb_ref, st_ref,
                h_ref, so_ref, wdo_ref, carry_ref, wgs_ref, wus_ref,
                *, nm, nq, bt, rb, last_half_valid, wd_valid_blocks, prompt_end, sample_starts, seq_len):
    p = pl.program_id(0)
    i = pl.program_id(1)
    bm = x_ref.shape[0]
    crows = wga_ref.shape[0]

    def cast_next_pair():
        rows = pl.ds(pl.multiple_of(i * crows, crows), crows)
        for half, (g_ref, u_ref) in enumerate(((wga_ref, wua_ref), (wgb_ref, wub_ref))):
            wgs_ref[p % 2, half, rows, :] = g_ref[...].astype(BF16)
            wus_ref[p % 2, half, rows, :] = u_ref[...].astype(BF16)

    @pl.when(p == 0)
    def _():
        cast_next_pair()

    @pl.when(p > 0)
    def _():
        slot = (p + 1) % 2
        rows8 = lax.broadcasted_iota(jnp.int32, (SUBLANES, bt), 0)
        is_last = i == nm - 1
        drop = [None, None if last_half_valid else (p == nq)]
        cols = [slice(half * bt, (half + 1) * bt) for half in range(2)]
        w0 = [cw_ref[0:1, c] for c in cols]
        w1 = [cw_ref[1:2, c] for c in cols]
        w2 = [cw_ref[2:3, c] for c in cols]
        cb = [cb_ref[:, c] for c in cols]

        def act(half, a_cur, a_p1, a_p2, u_cur):
            c = cb[half] + a_p2 * w0[half]
            c = c + a_p1 * w1[half]
            c = c + a_cur * w2[half]
            out = c * jax.nn.sigmoid(c) * u_cur
            if drop[half] is not None:
                out = jnp.where(drop[half], 0.0, out)
            return out.astype(BF16)

        prev8 = [jnp.where(i == 0, 0.0, carry_ref[:, c]) for c in cols]
        for r0 in range(0, bm, rb):
            xs = x_ref[r0:r0 + rb, :]
            for half in range(2):
                c = cols[half]
                a = jnp.dot(xs, wgs_ref[slot, half], preferred_element_type=F32)
                u = jnp.dot(xs, wus_ref[slot, half], preferred_element_type=F32)
                h_ref[r0:r0 + rb, c] = act(half, a, pltpu.roll(a, 1, 0), pltpu.roll(a, 2, 0), u)

                def patch(off, before8):
                    blk = a[off:off + SUBLANES]
                    p1 = jnp.where(rows8 < 1, pltpu.roll(before8, 1, 0), pltpu.roll(blk, 1, 0))
                    p2 = jnp.where(rows8 < 2, pltpu.roll(before8, 2, 0), pltpu.roll(blk, 2, 0))
                    h_ref[r0 + off:r0 + off + SUBLANES, c] = act(half, blk, p1, p2, u[off:off + SUBLANES])

                patch(0, prev8[half])
                for s, start in enumerate(sample_starts):
                    off = start - r0
                    if SUBLANES <= off < rb:
                        patch(off, jnp.where(is_last, st_ref[s, :, c], a[off - SUBLANES:off]))
                    end = off + seq_len
                    if SUBLANES <= end <= rb:
                        so_ref[1 + s, :, c] = a[end - SUBLANES:end]
                if SUBLANES <= prompt_end - r0 <= rb:
                    so_ref[0, :, c] = a[prompt_end - r0 - SUBLANES:prompt_end - r0]
                prev8[half] = a[rb - SUBLANES:rb]
        for half in range(2):
            carry_ref[:, cols[half]] = prev8[half]

        cast_next_pair()
        wd_rows_valid = (p - 1) * nm + i < wd_valid_blocks
        wdo_ref[...] = jnp.where(wd_rows_valid, wdc_ref[...], 0.0).astype(BF16)


def ffn_gate_up(xb, wg, wu, wd, conv_w, conv_b, state8, layer, *, bm, bt, rb, n_prompt_rows, seq_len):
    m, d = xb.shape
    n = wg.shape[2]
    nm, n_tiles = m // bm, n // bt
    nq = -(-n_tiles // 2)
    n_pad = nq * 2 * bt
    crows = d // nm
    drows = 2 * bt // nm
    n_seq = state8.shape[0]
    tile0 = (nm - 1) * bm
    prompt_end = n_prompt_rows - tile0
    sample_starts = tuple(prompt_end + s * seq_len for s in range(n_seq))
    assert m % bm == 0 and n % bt == 0 and bm % rb == 0 and rb % SUBLANES == 0
    assert d % nm == 0 and crows % SUBLANES == 0 and (2 * bt) % nm == 0 and drows % SUBLANES == 0
    assert n % drows == 0 and conv_w.shape[2] == n_pad and conv_b.shape[2] == n_pad and state8.shape[2] == n_pad
    assert m == n_prompt_rows + n_seq * seq_len and SUBLANES <= prompt_end
    assert all(st % SUBLANES == 0 and st % rb != 0 for st in sample_starts)
    est = 2 * (_nbytes((bm, d), BF16) + 4 * _nbytes((crows, bt), F32) + _nbytes((bm, 2 * bt), BF16))
    est += 2 * (_nbytes((drows, d), F32) + _nbytes((drows, d), BF16))
    est += 8 * _nbytes((d, bt), BF16)
    est += 6 * _nbytes((rb, bt), F32) + 4 * _nbytes((n_seq + 1, SUBLANES, 2 * bt), F32)
    row = lambda p, i: jnp.where(p > 0, i, 0)
    jq = lambda p: jnp.maximum(p - 1, 0)
    chunk = lambda p, i: jnp.where(p < nq, i, nm - 1)
    tile_a = lambda p: 2 * jnp.minimum(p, nq - 1)
    tile_b = lambda p: jnp.minimum(tile_a(p) + 1, n_tiles - 1)
    wspec_a = pl.BlockSpec((None, crows, bt), lambda p, i: (layer, chunk(p, i), tile_a(p)))
    wspec_b = pl.BlockSpec((None, crows, bt), lambda p, i: (layer, chunk(p, i), tile_b(p)))
    dblock = lambda p, i: jq(p) * nm + row(p, i)
    wd_valid_blocks = n // drows
    pair = lambda p, i: (layer, 0, jq(p))
    return pl.pallas_call(
        functools.partial(_ffn_kernel, nm=nm, nq=nq, bt=bt, rb=rb, last_half_valid=n_tiles % 2 == 0,
                          wd_valid_blocks=wd_valid_blocks, prompt_end=prompt_end,
                          sample_starts=sample_starts, seq_len=seq_len),
        grid=(nq + 1, nm),
        in_specs=[pl.BlockSpec((bm, d), lambda p, i: (row(p, i), 0)),
                  wspec_a, wspec_b, wspec_a, wspec_b,
                  pl.BlockSpec((None, drows, d),
                               lambda p, i: (layer, jnp.minimum(dblock(p, i), wd_valid_blocks - 1), 0)),
                  pl.BlockSpec((None, CONV_W, 2 * bt), pair),
                  pl.BlockSpec((None, 1, 2 * bt), pair),
                  pl.BlockSpec((n_seq, SUBLANES, 2 * bt), lambda p, i: (0, 0, jq(p)))],
        out_specs=[pl.BlockSpec((bm, 2 * bt), lambda p, i: (row(p, i), jq(p))),
                   pl.BlockSpec((n_seq + 1, SUBLANES, 2 * bt), lambda p, i: (0, 0, jq(p))),
                   pl.BlockSpec((drows, d), lambda p, i: (dblock(p, i), 0))],
        out_shape=[jax.ShapeDtypeStruct((m, n_pad), BF16),
                   jax.ShapeDtypeStruct((n_seq + 1, SUBLANES, n_pad), F32),
                   jax.ShapeDtypeStruct((n_pad, d), BF16)],
        scratch_shapes=[pltpu.VMEM((SUBLANES, 2 * bt), F32),
                        pltpu.VMEM((2, 2, d, bt), BF16),
                        pltpu.VMEM((2, 2, d, bt), BF16)],
        compiler_params=pltpu.CompilerParams(
            dimension_semantics=("arbitrary", "arbitrary"), vmem_limit_bytes=_vmem_limit(est)),
        name="ffn_gate_up",
    )(xb, wg, wg, wu, wu, wd, conv_w, conv_b, state8)


def kernel(x_prompt, x_sample, state_mlstm_C, state_mlstm_n, state_mlstm_m, state_hgrn_S, state_ffn_conv,
           ln_g, ln_b, w_a_in, b_a_gate, a_norm_w, w_a_out, w_b_in, b_lower, b_norm_w, w_b_out,
           w_ffn_gate, w_ffn_up, w_ffn_down, ffn_conv_w, ffn_conv_b):
    bp, tp, d = x_prompt.shape
    bs, ts, _ = x_sample.shape
    mp = bp * tp
    ms = bs * ts
    m = mp + ms
    assert bp == 1 and ts == CHUNK and d == D_MODEL
    l_prompt = 128

    x = jnp.concatenate([x_prompt.reshape(mp, d), x_sample.reshape(ms, d)], axis=0)
    xb = x.astype(BF16)
    w_a_in_t = jnp.swapaxes(w_a_in, 1, 2)
    ff_pad = -D_FF % (2 * FFN_BT)
    conv_w_p = jnp.pad(ffn_conv_w, ((0, 0), (0, 0), (0, ff_pad)))
    conv_b_p = jnp.pad(ffn_conv_b, ((0, 0), (0, ff_pad))).reshape(DEPTH, 1, D_FF + ff_pad)

    p_c, p_n, p_m, p_s, p_conv = [], [], [], [], []
    s_c, s_n, s_m, s_s, s_conv = [], [], [], [], []
    for l in range(DEPTH):
        j = l // 2
        if l % 2 == 0:
            w_gate = jnp.pad(w_a_in_t[j, A_MAIN:, :], ((0, LANES - 2 * NH_A), (0, 0)))
            b_gate = jnp.pad(b_a_gate[j], (0, LANES - 2 * NH_A)).reshape(1, LANES)
            proj = matmul_wcast(xb, w_a_in_t, j, n_cols=A_MAIN, w_t=True, bm=1088, bn=1024,
                                name="mlstm_in_proj")
            gates = mlstm_gates(xb, w_gate, b_gate)
            y_p, c_p, n_p, m_p = mlstm_scan(proj, gates, a_norm_w[j], row0=0, n_seq=1,
                                            n_chunk=mp // l_prompt, L=l_prompt)
            y_s, c_s2, n_s2, m_s2 = mlstm_scan(
                proj, gates, a_norm_w[j], row0=mp, n_seq=bs, n_chunk=1, L=ts,
                init=(state_mlstm_C, state_mlstm_n, state_mlstm_m, j))
            p_c.append(c_p); p_n.append(n_p); p_m.append(m_p)
            s_c.append(c_s2); s_n.append(n_s2); s_m.append(m_s2)
            y_mix = jnp.concatenate([y_p, y_s], axis=0)
            w_out = w_a_out
        else:
            proj = matmul_wcast(xb, w_b_in, j, bm=1088, bn=1024, name="hgrn_in_proj")
            y_mix, st = hgrn_scan(proj, b_lower, b_norm_w[j], state_hgrn_S, layer_j=j,
                                  n_prompt_chunks=mp // CHUNK)
            p_s.append(st[:1]); s_s.append(st[1:])
            w_out = w_b_out
        y = matmul_wcast(y_mix, w_out, j, bm=544, bn=1024, res=x, alpha=ALPHA, name="mixer_out_proj")
        x, xb = layer_norm(y, ln_g[l, 0], ln_b[l, 0])

        st8 = jnp.pad(state_ffn_conv[l], ((0, 0), (SUBLANES - (CONV_W - 1), 0), (0, ff_pad)))
        hmid, so, wd_b = ffn_gate_up(xb, w_ffn_gate, w_ffn_up, w_ffn_down, conv_w_p, conv_b_p, st8, l,
                                     bm=1088, bt=FFN_BT, rb=272, n_prompt_rows=mp, seq_len=ts)
        conv_new = so[:, SUBLANES - (CONV_W - 1):, :D_FF]
        p_conv.append(conv_new[:1]); s_conv.append(conv_new[1:])
        y = matmul_xres(hmid, wd_b, x, bm=544, bn=512, alpha=ALPHA, name="ffn_down_proj")
        if l < DEPTH - 1:
            x, xb = layer_norm(y, ln_g[l, 1], ln_b[l, 1])

    (y_prompt,) = layer_norm(y, ln_g[DEPTH - 1, 1], ln_b[DEPTH - 1, 1], rows=mp, emit_bf16=False)
    (y_sample,) = layer_norm(y, ln_g[DEPTH - 1, 1], ln_b[DEPTH - 1, 1], row0=mp, emit_bf16=False)
    y_prompt = y_prompt.reshape(bp, tp, d)
    y_sample = y_sample.reshape(bs, ts, d)
    return (y_prompt, y_sample,
            jnp.stack(p_c), jnp.stack(p_n), jnp.stack(p_m), jnp.stack(p_s), jnp.stack(p_conv),
            jnp.stack(s_c), jnp.stack(s_n), jnp.stack(s_m), jnp.stack(s_s), jnp.stack(s_conv))
```

```python
import functools

import jax
import jax.numpy as jnp
from jax import lax
from jax.experimental import pallas as pl
from jax.experimental.pallas import tpu as pltpu

F32 = jnp.float32
BF16 = jnp.bfloat16

D_MODEL = 4096
DEPTH = 4
CHUNK = 64
NH_A = 8
DK_A = 256
DV_A = 512
QK_A = NH_A * DK_A
VD_A = NH_A * DV_A
A_MAIN = 2 * QK_A + 2 * VD_A
GATE_SOFTCAP = 15.0
NH_B = 32
DK_B = 128
DV_B = 128
D_FF = 11008
FFN_BT = 256
CONV_W = 3
ALPHA = (2 * DEPTH) ** 0.25
LN_EPS = 1e-5
HEAD_EPS = 1e-6

LANES = 128
SUBLANES = 8
VMEM_LIMIT_CAP = 58 * 1024 * 1024


def _vmem_limit(nbytes):
    return int(min(VMEM_LIMIT_CAP, nbytes * 1.2 + (6 << 20)))


def _nbytes(shape, dtype):
    n = 1
    for s in shape:
        n *= s
    return n * jnp.dtype(dtype).itemsize


def _finish(acc, r_ref, o_ref, alpha):
    if r_ref is not None:
        acc = alpha * r_ref[...] + acc
    o_ref[...] = acc.astype(o_ref.dtype)


def _mm_wcast_kernel(*refs, nn, nm, w_t, has_res, alpha):
    if has_res:
        x_ref, wc_ref, r_ref, o_ref, wb_ref = refs
    else:
        x_ref, wc_ref, o_ref, wb_ref = refs
        r_ref = None
    p = pl.program_id(0)
    i = pl.program_id(1)
    crows = wc_ref.shape[0]

    @pl.when(p < nn)
    def _():
        wb_ref[p % 2, pl.ds(pl.multiple_of(i * crows, crows), crows), :] = wc_ref[...].astype(BF16)

    @pl.when(p > 0)
    def _():
        wb = wb_ref[(p + 1) % 2]
        if w_t:
            acc = lax.dot_general(x_ref[...], wb, (((1,), (1,)), ((), ())), preferred_element_type=F32)
        else:
            acc = jnp.dot(x_ref[...], wb, preferred_element_type=F32)
        _finish(acc, r_ref, o_ref, alpha)


def matmul_wcast(x, w, layer, *, bm, bn, n_cols=None, w_t=False, res=None, alpha=1.0, out_dtype=F32,
                 name="mm"):
    m, kdim = x.shape
    n = w.shape[1 if w_t else 2] if n_cols is None else n_cols
    nm, nn = m // bm, n // bn
    crows = (bn if w_t else kdim) // nm
    assert m % bm == 0 and n % bn == 0 and (bn if w_t else kdim) % nm == 0 and crows % SUBLANES == 0
    row = lambda p, i: jnp.where(p > 0, i, 0)
    xi = lambda p, i: (row(p, i), 0)
    oi = lambda p, i: (row(p, i), jnp.maximum(p - 1, 0))
    chunk = lambda p, i: jnp.where(p < nn, i, nm - 1)
    tile = lambda p: jnp.minimum(p, nn - 1)
    if w_t:
        wspec = pl.BlockSpec((None, crows, kdim), lambda p, i: (layer, tile(p) * nm + chunk(p, i), 0))
        wb_shape = (2, bn, kdim)
    else:
        wspec = pl.BlockSpec((None, crows, bn), lambda p, i: (layer, chunk(p, i), tile(p)))
        wb_shape = (2, kdim, bn)
    in_specs = [pl.BlockSpec((bm, kdim), xi), wspec]
    args = [x, w]
    est = 2 * (_nbytes((bm, kdim), BF16) + _nbytes(wspec.block_shape[1:], F32) + _nbytes((bm, bn), out_dtype))
    est += _nbytes((bm, bn), F32) + _nbytes(wb_shape, BF16)
    if res is not None:
        in_specs.append(pl.BlockSpec((bm, bn), oi))
        args.append(res)
        est += 2 * _nbytes((bm, bn), res.dtype)
    return pl.pallas_call(
        functools.partial(_mm_wcast_kernel, nn=nn, nm=nm, w_t=w_t, has_res=res is not None, alpha=alpha),
        grid=(nn + 1, nm),
        in_specs=in_specs,
        out_specs=pl.BlockSpec((bm, bn), oi),
        out_shape=jax.ShapeDtypeStruct((m, n), out_dtype),
        scratch_shapes=[pltpu.VMEM(wb_shape, BF16)],
        compiler_params=pltpu.CompilerParams(
            dimension_semantics=("arbitrary", "arbitrary"), vmem_limit_bytes=_vmem_limit(est)),
        name=name,
    )(*args)


def _mm_kernel(x_ref, w_ref, r_ref, o_ref, *, alpha):
    _finish(jnp.dot(x_ref[...], w_ref[...], preferred_element_type=F32), r_ref, o_ref, alpha)


def matmul_xres(x, w, res, *, bm, bn, alpha, name="mm"):
    m, kdim = x.shape
    n = w.shape[1]
    assert m % bm == 0 and n % bn == 0
    oi = lambda i, j: (i, j)
    est = 2 * (_nbytes((bm, kdim), BF16) + _nbytes((kdim, bn), BF16) + 2 * _nbytes((bm, bn), F32))
    est += _nbytes((bm, bn), F32)
    return pl.pallas_call(
        functools.partial(_mm_kernel, alpha=alpha),
        grid=(m // bm, n // bn),
        in_specs=[pl.BlockSpec((bm, kdim), lambda i, j: (i, 0)),
                  pl.BlockSpec((kdim, bn), lambda i, j: (0, j)),
                  pl.BlockSpec((bm, bn), oi)],
        out_specs=pl.BlockSpec((bm, bn), oi),
        out_shape=jax.ShapeDtypeStruct((m, n), F32),
        compiler_params=pltpu.CompilerParams(
            dimension_semantics=("arbitrary", "arbitrary"), vmem_limit_bytes=_vmem_limit(est)),
        name=name,
    )(x, w, res)


def _ln_kernel(y_ref, g_ref, b_ref, o_ref, ob_ref=None):
    y = y_ref[...]
    mu = jnp.mean(y, axis=-1, keepdims=True)
    yc = y - mu
    var = jnp.mean(yc * yc, axis=-1, keepdims=True)
    out = yc * lax.rsqrt(var + LN_EPS) * g_ref[...] + b_ref[...]
    o_ref[...] = out
    if ob_ref is not None:
        ob_ref[...] = out.astype(BF16)


def layer_norm(y, g, b, *, bm=256, row0=0, rows=None, emit_bf16=True):
    m, d = y.shape
    rows = m - row0 if rows is None else rows
    assert row0 % bm == 0 and rows % bm == 0
    blk0 = row0 // bm
    row = pl.BlockSpec((bm, d), lambda i: (i, 0))
    vec = pl.BlockSpec((1, d), lambda i: (0, 0))
    est = 2 * (2 * _nbytes((bm, d), F32) + _nbytes((bm, d), BF16)) + 2 * _nbytes((bm, d), F32)
    out_specs = [row, row] if emit_bf16 else [row]
    out_shape = [jax.ShapeDtypeStruct((rows, d), F32)]
    if emit_bf16:
        out_shape.append(jax.ShapeDtypeStruct((rows, d), BF16))
    return pl.pallas_call(
        _ln_kernel,
        grid=(rows // bm,),
        in_specs=[pl.BlockSpec((bm, d), lambda i: (blk0 + i, 0)), vec, vec],
        out_specs=out_specs,
        out_shape=out_shape,
        compiler_params=pltpu.CompilerParams(
            dimension_semantics=("arbitrary",), vmem_limit_bytes=_vmem_limit(est)),
        name="layer_norm",
    )(y, g.reshape(1, d), b.reshape(1, d))


def _log_sigmoid(z):
    return jnp.minimum(z, 0.0) - jnp.log1p(jnp.exp(-jnp.abs(z)))


def _gates_kernel(x_ref, w_ref, b_ref, o_ref):
    g = lax.dot_general(x_ref[...], w_ref[...].astype(BF16), (((1,), (1,)), ((), ())),
                        preferred_element_type=F32)
    g = g + b_ref[...]
    sc = GATE_SOFTCAP * jnp.tanh(g / GATE_SOFTCAP)
    lane = lax.broadcasted_iota(jnp.int32, sc.shape, 1)
    o_ref[...] = jnp.where(lane < NH_A, sc, _log_sigmoid(sc))


def mlstm_gates(xb, w_gate_t, b_gate, *, bm=1088):
    m, d = xb.shape
    est = 2 * (_nbytes((bm, d), BF16) + 2 * _nbytes((LANES, d), F32) + _nbytes((bm, LANES), F32))
    return pl.pallas_call(
        _gates_kernel,
        grid=(m // bm,),
        in_specs=[pl.BlockSpec((bm, d), lambda i: (i, 0)),
                  pl.BlockSpec((LANES, d), lambda i: (0, 0)),
                  pl.BlockSpec((1, LANES), lambda i: (0, 0))],
        out_specs=pl.BlockSpec((bm, LANES), lambda i: (i, 0)),
        out_shape=jax.ShapeDtypeStruct((m, LANES), F32),
        compiler_params=pltpu.CompilerParams(
            dimension_semantics=("arbitrary",), vmem_limit_bytes=_vmem_limit(est)),
        name="mlstm_gates",
    )(xb, w_gate_t, b_gate)


def _mlstm_kernel(*refs, L, has_init):
    if has_init:
        (q_ref, k_ref, v_ref, o_ref, gc_ref, gr_ref, nw_ref, c0_ref, n0_ref, m0_ref, y_prev_ref,
         y_ref, ct_ref, nt_ref, mt_ref, c_s, n_s, m_s) = refs
    else:
        (q_ref, k_ref, v_ref, o_ref, gc_ref, gr_ref, nw_ref,
         y_ref, ct_ref, nt_ref, mt_ref, c_s, n_s, m_s) = refs
    c = pl.program_id(1)
    nc = pl.num_programs(1)

    @pl.when(c == 0)
    def _():
        if has_init:
            c_s[...] = c0_ref[0]
            n_s[...] = n0_ref[0]
            m_s[...] = m0_ref[0]
        else:
            c_s[...] = jnp.zeros_like(c_s)
            n_s[...] = jnp.zeros_like(n_s)
            m_s[...] = jnp.zeros_like(m_s)

    t_idx = lax.broadcasted_iota(jnp.int32, (L, L), 0)
    s_idx = lax.broadcasted_iota(jnp.int32, (L, L), 1)
    causal = s_idx <= t_idx
    lane = lax.broadcasted_iota(jnp.int32, (L, LANES), 1)
    gcol = gc_ref[...]

    def head(h, carry):
        qs = pl.ds(pl.multiple_of(h * DK_A, DK_A), DK_A)
        vs = pl.ds(pl.multiple_of(h * DV_A, DV_A), DV_A)
        q = q_ref[:, qs]
        k = k_ref[:, qs] * (DK_A ** -0.5)
        v = v_ref[:, vs]
        qb = q.astype(BF16)
        kb = k.astype(BF16)
        vb = v.astype(BF16)
        ig_r = gr_ref[h, 0]
        lf_r = gr_ref[NH_A + h, 0]
        ig_c = jnp.sum(jnp.where(lane == h, gcol, 0.0), axis=1, keepdims=True)
        lf_c = jnp.sum(jnp.where(lane == NH_A + h, gcol, 0.0), axis=1, keepdims=True)
        f_c = jnp.sum(jnp.where(causal, lf_r, 0.0), axis=1, keepdims=True)
        f_r = jnp.sum(jnp.where(t_idx <= s_idx, lf_c, 0.0), axis=0, keepdims=True)
        f_end = jnp.sum(lf_r, axis=1, keepdims=True)
        m_prev = m_s[h]
        c_prev = c_s[h]
        n_prev = n_s[h]

        d = jnp.where(causal, f_c - f_r + ig_r, -jnp.inf)
        g = f_c + m_prev
        m_t = jnp.maximum(g, jnp.max(d, axis=1, keepdims=True))
        w = jnp.exp(d - m_t)
        inter = jnp.exp(g - m_t)
        s = lax.dot_general(qb, kb, (((1,), (1,)), ((), ())), preferred_element_type=F32) * w
        num = jnp.dot(s.astype(BF16), vb, preferred_element_type=F32)
        num = num + inter * jnp.dot(qb, c_prev.astype(BF16), preferred_element_type=F32)
        den = jnp.sum(s, axis=1, keepdims=True) + inter * jnp.sum(q * n_prev, axis=1, keepdims=True)
        hh = num / jnp.maximum(jnp.abs(den), jnp.exp(-m_t))

        d_end_r = f_end - f_r + ig_r
        d_end_c = f_end - f_c + ig_c
        m_new = jnp.maximum(f_end + m_prev, jnp.max(d_end_r, axis=1, keepdims=True))
        w_end_c = jnp.exp(d_end_c - m_new)
        decay = jnp.exp(f_end + m_prev - m_new)
        kw = k * w_end_c
        c_s[h] = decay * c_prev + lax.dot_general(
            kw.astype(BF16), vb, (((0,), (0,)), ((), ())), preferred_element_type=F32)
        n_s[h] = decay * n_prev + jnp.sum(kw, axis=0, keepdims=True)
        m_s[h] = m_new

        mu = jnp.mean(hh, axis=1, keepdims=True)
        hc = hh - mu
        hn = hc * lax.rsqrt(jnp.mean(hc * hc, axis=1, keepdims=True) + HEAD_EPS)
        hn = hn * nw_ref[:, vs]
        y_ref[:, vs] = (jax.nn.sigmoid(o_ref[:, vs]) * hn).astype(BF16)
        return carry

    lax.fori_loop(0, NH_A, head, 0, unroll=2)

    @pl.when(c == nc - 1)
    def _():
        ct_ref[0] = c_s[...]
        nt_ref[0] = n_s[...]
        mt_ref[0] = m_s[...]


def mlstm_scan(proj, gates, norm_w, *, row0, n_seq, n_chunk, L, init=None):
    m = proj.shape[0]
    rb0 = row0 // L
    rows = n_seq * n_chunk * L
    g_rows = lax.slice_in_dim(gates, row0, row0 + rows, axis=0)
    g_t = g_rows[:, :2 * NH_A].T.reshape(2 * NH_A, n_seq * n_chunk, 1, L)
    rmap = lambda s, c: (rb0 + s * n_chunk + c)
    in_specs = [
        pl.BlockSpec((L, QK_A), lambda s, c: (rmap(s, c), 0)),
        pl.BlockSpec((L, QK_A), lambda s, c: (rmap(s, c), 1)),
        pl.BlockSpec((L, VD_A), lambda s, c: (rmap(s, c), 1)),
        pl.BlockSpec((L, VD_A), lambda s, c: (rmap(s, c), 2)),
        pl.BlockSpec((L, LANES), lambda s, c: (rmap(s, c), 0)),
        pl.BlockSpec((2 * NH_A, 1, 1, L), lambda s, c: (0, s * n_chunk + c, 0, 0)),
        pl.BlockSpec((1, VD_A), lambda s, c: (0, 0)),
    ]
    args = [proj, proj, proj, proj, gates, g_t, norm_w.reshape(1, VD_A)]
    state_specs = [
        pl.BlockSpec((1, NH_A, DK_A, DV_A), lambda s, c: (s, 0, 0, 0)),
        pl.BlockSpec((1, NH_A, 1, DK_A), lambda s, c: (s, 0, 0, 0)),
        pl.BlockSpec((1, NH_A, 1, 1), lambda s, c: (s, 0, 0, 0)),
    ]
    io_alias = {}
    if init is not None:
        c0, n0, m0, layer, y_prev = init
        nl = c0.shape[0]
        in_specs += [
            pl.BlockSpec((None, 1, NH_A, DK_A, DV_A), lambda s, c: (layer, s, 0, 0, 0)),
            pl.BlockSpec((None, 1, NH_A, 1, DK_A), lambda s, c: (layer, s, 0, 0, 0)),
            pl.BlockSpec((None, 1, NH_A, 1, 1), lambda s, c: (layer, s, 0, 0, 0)),
            pl.BlockSpec(memory_space=pl.ANY),
        ]
        args += [c0, n0.reshape(nl, n_seq, NH_A, 1, DK_A), m0.reshape(nl, n_seq, NH_A, 1, 1), y_prev]
        io_alias = {len(args) - 1: 0}
    est = 2 * (2 * _nbytes((L, QK_A), F32) + 2 * _nbytes((L, VD_A), F32) + _nbytes((L, VD_A), BF16))
    est += (3 if init is None else 5) * _nbytes((NH_A, DK_A, DV_A), F32)
    est += 8 * _nbytes((L, max(L, DV_A)), F32)
    y, ct, nt, mt = pl.pallas_call(
        functools.partial(_mlstm_kernel, L=L, has_init=init is not None),
        grid=(n_seq, n_chunk),
        in_specs=in_specs,
        out_specs=[pl.BlockSpec((L, VD_A), lambda s, c: (rmap(s, c), 0))] + state_specs,
        out_shape=[
            jax.ShapeDtypeStruct((m, VD_A), BF16),
            jax.ShapeDtypeStruct((n_seq, NH_A, DK_A, DV_A), F32),
            jax.ShapeDtypeStruct((n_seq, NH_A, 1, DK_A), F32),
            jax.ShapeDtypeStruct((n_seq, NH_A, 1, 1), F32),
        ],
        scratch_shapes=[
            pltpu.VMEM((NH_A, DK_A, DV_A), F32),
            pltpu.VMEM((NH_A, 1, DK_A), F32),
            pltpu.VMEM((NH_A, 1, 1), F32),
        ],
        input_output_aliases=io_alias,
        compiler_params=pltpu.CompilerParams(
            dimension_semantics=("arbitrary", "arbitrary"), vmem_limit_bytes=_vmem_limit(est)),
        name="mlstm_scan_init" if init is not None else "mlstm_scan",
    )(*args)
    return y, ct, nt.reshape(n_seq, NH_A, DK_A), mt.reshape(n_seq, NH_A)


def _hgrn_kernel(q_ref, f_ref, i_ref, g_ref, bl_ref, nw_ref, s0_ref, y_ref, st_ref, s_s,
                 *, layer_j, n_prompt_chunks, pair_unroll):
    L = CHUNK
    W = 2 * DK_B
    n_pairs = NH_B // 2
    g_id = pl.program_id(0)

    @pl.when(g_id == 0)
    def _():
        s_s[...] = jnp.zeros_like(s_s)

    @pl.when(g_id >= n_prompt_chunks)
    def _():
        def load(hp, carry):
            s_s[hp] = jnp.concatenate([s0_ref[0, 2 * hp].T, s0_ref[0, 2 * hp + 1].T], axis=1)
            return carry
        lax.fori_loop(0, n_pairs, load, 0)

    row8 = lax.broadcasted_iota(jnp.int32, (SUBLANES, W), 0)
    t_idx = lax.broadcasted_iota(jnp.int32, (2 * L, 2 * L), 0)
    s_idx = lax.broadcasted_iota(jnp.int32, (2 * L, 2 * L), 1)
    levels = [1, 2, 4, 8, 16, 32]
    n_grp = L // SUBLANES
    groups = lambda arr: [arr[b * SUBLANES:(b + 1) * SUBLANES] for b in range(n_grp)]
    upper8 = {mm: (row8 & mm) != 0 for mm in levels if mm < SUBLANES}
    pair = {mm: ((t_idx ^ s_idx) < 2 * mm) & ((t_idx & mm) != 0) & ((s_idx & mm) == 0) for mm in levels}
    diag = t_idx == s_idx
    nt_dims = (((1,), (1,)), ((), ()))
    tn_dims = (((0,), (0,)), ((), ()))
    zero_half = jnp.zeros((L, DK_B), BF16)

    def block_diag(xw):
        top = jnp.concatenate([xw[:, :DK_B], zero_half], axis=1)
        bot = jnp.concatenate([zero_half, xw[:, DK_B:]], axis=1)
        return jnp.concatenate([top, bot], axis=0)

    def stack_heads(xw):
        return jnp.concatenate([xw[:, :DK_B], xw[:, DK_B:]], axis=0)

    def head(hp, carry):
        sl = pl.ds(pl.multiple_of(hp * W, W), W)
        q = q_ref[:, sl]
        fz = f_ref[:, sl]
        inp = i_ref[:, sl]
        gate = g_ref[:, sl]
        bl = bl_ref[:, sl]
        e = jnp.exp(bl - jnp.max(bl, axis=0, keepdims=True))
        p = e / jnp.sum(e, axis=0, keepdims=True)
        lb = jnp.zeros((1, W), F32)
        for r in range(1, layer_j + 1):
            lb = lb + p[r:r + 1]
        sig = jax.nn.sigmoid(fz)
        oml = 1.0 - lb
        logf = jnp.log(lb + oml * sig)
        k = oml * (1.0 - sig)

        ib = stack_heads(inp.astype(BF16))
        s_acc = jnp.where(
            diag, lax.dot_general(block_diag(q.astype(BF16)), block_diag(k.astype(BF16)), nt_dims,
                                  preferred_element_type=F32), 0.0)
        q8, k8 = groups(q), groups(k)
        p_sum = groups(logf)
        r_sum = [jnp.zeros((SUBLANES, W), F32)] * n_grp
        g_sum = list(p_sum)
        for mm in levels:
            xs, g_new = [], []
            for b in range(n_grp):
                if mm < SUBLANES:
                    up = upper8[mm]
                    sib = jnp.where(up, pltpu.roll(g_sum[b], mm, 0), pltpu.roll(g_sum[b], SUBLANES - mm, 0))
                    xs.append(jnp.where(up, q8[b], k8[b]) * jnp.exp(jnp.where(up, p_sum[b], r_sum[b])))
                    p_sum[b] = p_sum[b] + jnp.where(up, sib, 0.0)
                    r_sum[b] = r_sum[b] + jnp.where(up, 0.0, sib)
                else:
                    bit = mm // SUBLANES
                    sib = g_sum[b ^ bit]
                    if b & bit:
                        xs.append(q8[b] * jnp.exp(p_sum[b]))
                        p_sum[b] = p_sum[b] + sib
                    else:
                        xs.append(k8[b] * jnp.exp(r_sum[b]))
                        r_sum[b] = r_sum[b] + sib
                g_new.append(g_sum[b] + sib)
            g_sum = g_new
            x = block_diag(jnp.concatenate(xs, axis=0).astype(BF16))
            s_m = lax.dot_general(x, x, nt_dims, preferred_element_type=F32)
            s_acc = jnp.where(pair[mm], s_m, s_acc)
        p_sum = jnp.concatenate(p_sum, axis=0)
        r_sum = jnp.concatenate(r_sum, axis=0)
        g_sum = g_sum[0]
        s_prev = s_s[hp]
        o = lax.dot_general(block_diag((q * jnp.exp(p_sum)).astype(BF16)), s_prev.astype(BF16), nt_dims,
                            preferred_element_type=F32)
        o = o + jnp.dot(s_acc.astype(BF16), ib, preferred_element_type=F32)
        k_end = block_diag((k * jnp.exp(r_sum)).astype(BF16))
        s_s[hp] = jnp.exp(g_sum[0:1]) * s_prev + lax.dot_general(ib, k_end, tn_dims, preferred_element_type=F32)

        o = o * lax.rsqrt(jnp.mean(o * o, axis=1, keepdims=True) + HEAD_EPS)
        post = nw_ref[:, sl] * (gate * jax.nn.sigmoid(gate))
        y_ref[:, sl] = jnp.concatenate([o[:L] * post[:, :DV_B], o[L:] * post[:, DV_B:]], axis=1).astype(BF16)
        return carry

    lax.fori_loop(0, n_pairs, head, 0, unroll=pair_unroll)

    @pl.when(g_id >= n_prompt_chunks - 1)
    def _():
        def store(hp, carry):
            st = s_s[hp]
            st_ref[0, 2 * hp] = st[:, :DK_B].T
            st_ref[0, 2 * hp + 1] = st[:, DK_B:].T
            return carry
        lax.fori_loop(0, n_pairs, store, 0)


def hgrn_scan(proj, b_lower, norm_w, s0, *, layer_j, n_prompt_chunks, pair_unroll=8):
    m = proj.shape[0]
    n_chunks = m // CHUNK
    n_seq = n_chunks - n_prompt_chunks + 1
    n_b = b_lower.shape[0]
    col = lambda cb: pl.BlockSpec((CHUNK, D_MODEL), lambda g: (g, cb))
    est = 2 * (4 * _nbytes((CHUNK, D_MODEL), F32) + _nbytes((CHUNK, D_MODEL), BF16))
    est += 5 * _nbytes((NH_B, DK_B, DV_B), F32)
    y, st = pl.pallas_call(
        functools.partial(_hgrn_kernel, layer_j=layer_j, n_prompt_chunks=n_prompt_chunks,
                          pair_unroll=pair_unroll),
        grid=(n_chunks,),
        in_specs=[col(0), col(1), col(2), col(3),
                  pl.BlockSpec((n_b, D_MODEL), lambda g: (0, 0)),
                  pl.BlockSpec((1, D_MODEL), lambda g: (0, 0)),
                  pl.BlockSpec((None, 1, NH_B, DK_B, DV_B),
                               lambda g: (layer_j, jnp.maximum(g - n_prompt_chunks, 0), 0, 0, 0))],
        out_specs=[pl.BlockSpec((CHUNK, D_MODEL), lambda g: (g, 0)),
                   pl.BlockSpec((1, NH_B, DK_B, DV_B),
                                lambda g: (jnp.maximum(g - (n_prompt_chunks - 1), 0), 0, 0, 0))],
        out_shape=[jax.ShapeDtypeStruct((m, D_MODEL), BF16),
                   jax.ShapeDtypeStruct((n_seq, NH_B, DK_B, DV_B), F32)],
        scratch_shapes=[pltpu.VMEM((NH_B // 2, DV_B, 2 * DK_B), F32)],
        compiler_params=pltpu.CompilerParams(
            dimension_semantics=("arbitrary",), vmem_limit_bytes=_vmem_limit(est)),
        name="hgrn_scan",
    )(proj, proj, proj, proj, b_lower, norm_w.reshape(1, D_MODEL), s0)
    return y, st


def _ffn_kernel(x_ref, wga_ref, wgb_ref, wua_ref, wub_ref, wdc_ref, cw_ref, cb_ref, st_ref,
                h_ref, so_ref, wdo_ref, carry_ref, wgs_ref, wus_ref,
                *, nm, nq, bt, rb, last_half_valid, wd_valid_blocks, prompt_end, sample_starts, seq_len):
    p = pl.program_id(0)
    i = pl.program_id(1)
    bm = x_ref.shape[0]
    crows = wga_ref.shape[0]

    def cast_next_pair():
        rows = pl.ds(pl.multiple_of(i * crows, crows), crows)
        for half, (g_ref, u_ref) in enumerate(((wga_ref, wua_ref), (wgb_ref, wub_ref))):
            wgs_ref[p % 2, half, rows, :] = g_ref[...].astype(BF16)
            wus_ref[p % 2, half, rows, :] = u_ref[...].astype(BF16)

    @pl.when(p == 0)
    def _():
        cast_next_pair()

    cols = [slice(half * bt, (half + 1) * bt) for half in range(2)]

    def compute(n_half, cast_ahead):
        slot = (p + 1) % 2
        rows8 = lax.broadcasted_iota(jnp.int32, (SUBLANES, bt), 0)
        is_last = i == nm - 1
        w0 = [cw_ref[0:1, c] for c in cols]
        w1 = [cw_ref[1:2, c] for c in cols]
        w2 = [cw_ref[2:3, c] for c in cols]
        cb = [cb_ref[:, c] for c in cols]

        def act(half, a_cur, a_p1, a_p2, u_cur):
            c = cb[half] + a_p2 * w0[half]
            c = c + a_p1 * w1[half]
            c = c + a_cur * w2[half]
            return (c * jax.nn.sigmoid(c) * u_cur).astype(BF16)

        prev8 = [jnp.where(i == 0, 0.0, carry_ref[:, c]) for c in cols]
        for r0 in range(0, bm, rb):
            xs = x_ref[r0:r0 + rb, :]
            for half in range(n_half):
                c = cols[half]
                a = jnp.dot(xs, wgs_ref[slot, half], preferred_element_type=F32)
                u = jnp.dot(xs, wus_ref[slot, half], preferred_element_type=F32)
                h_ref[r0:r0 + rb, c] = act(half, a, pltpu.roll(a, 1, 0), pltpu.roll(a, 2, 0), u)

                def patch(off, before8):
                    blk = a[off:off + SUBLANES]
                    p1 = jnp.where(rows8 < 1, pltpu.roll(before8, 1, 0), pltpu.roll(blk, 1, 0))
                    p2 = jnp.where(rows8 < 2, pltpu.roll(before8, 2, 0), pltpu.roll(blk, 2, 0))
                    h_ref[r0 + off:r0 + off + SUBLANES, c] = act(half, blk, p1, p2, u[off:off + SUBLANES])

                patch(0, prev8[half])
                for s, start in enumerate(sample_starts):
                    off = start - r0
                    if SUBLANES <= off < rb:
                        patch(off, jnp.where(is_last, st_ref[s, :, c], a[off - SUBLANES:off]))
                    end = off + seq_len
                    if SUBLANES <= end <= rb:
                        so_ref[1 + s, :, c] = a[end - SUBLANES:end]
                if SUBLANES <= prompt_end - r0 <= rb:
                    so_ref[0, :, c] = a[prompt_end - r0 - SUBLANES:prompt_end - r0]
                prev8[half] = a[rb - SUBLANES:rb]
        for half in range(n_half):
            carry_ref[:, cols[half]] = prev8[half]

        if cast_ahead:
            cast_next_pair()
        wd_rows_valid = (p - 1) * nm + i < wd_valid_blocks
        wdo_ref[...] = jnp.where(wd_rows_valid, wdc_ref[...], 0.0).astype(BF16)

    @pl.when((p > 0) & (p < nq))
    def _():
        compute(2, True)

    @pl.when(p == nq)
    def _():
        if last_half_valid:
            compute(2, False)
        else:
            compute(1, False)
            h_ref[:, cols[1]] = jnp.zeros((bm, bt), BF16)
            so_ref[:, :, cols[1]] = jnp.zeros((so_ref.shape[0], SUBLANES, bt), F32)


def ffn_gate_up(xb, wg, wu, wd, conv_w, conv_b, state8, layer, *, bm, bt, rb, n_prompt_rows, seq_len):
    m, d = xb.shape
    n = wg.shape[2]
    nm, n_tiles = m // bm, n // bt
    nq = -(-n_tiles // 2)
    n_pad = nq * 2 * bt
    crows = d // nm
    drows = 2 * bt // nm
    n_seq = state8.shape[0]
    tile0 = (nm - 1) * bm
    prompt_end = n_prompt_rows - tile0
    sample_starts = tuple(prompt_end + s * seq_len for s in range(n_seq))
    assert m % bm == 0 and n % bt == 0 and bm % rb == 0 and rb % SUBLANES == 0
    assert d % nm == 0 and crows % SUBLANES == 0 and (2 * bt) % nm == 0 and drows % SUBLANES == 0
    assert n % drows == 0 and conv_w.shape[2] == n_pad and conv_b.shape[2] == n_pad and state8.shape[2] == n_pad
    assert m == n_prompt_rows + n_seq * seq_len and SUBLANES <= prompt_end
    assert all(st % SUBLANES == 0 and st % rb != 0 for st in sample_starts)
    est = 2 * (_nbytes((bm, d), BF16) + 4 * _nbytes((crows, bt), F32) + _nbytes((bm, 2 * bt), BF16))
    est += 2 * (_nbytes((drows, d), F32) + _nbytes((drows, d), BF16))
    est += 8 * _nbytes((d, bt), BF16)
    est += 6 * _nbytes((rb, bt), F32) + 4 * _nbytes((n_seq + 1, SUBLANES, 2 * bt), F32)
    row = lambda p, i: jnp.where(p > 0, i, 0)
    jq = lambda p: jnp.maximum(p - 1, 0)
    chunk = lambda p, i: jnp.where(p < nq, i, nm - 1)
    tile_a = lambda p: 2 * jnp.minimum(p, nq - 1)
    tile_b = lambda p: jnp.minimum(tile_a(p) + 1, n_tiles - 1)
    wspec_a = pl.BlockSpec((None, crows, bt), lambda p, i: (layer, chunk(p, i), tile_a(p)))
    wspec_b = pl.BlockSpec((None, crows, bt), lambda p, i: (layer, chunk(p, i), tile_b(p)))
    dblock = lambda p, i: jq(p) * nm + row(p, i)
    wd_valid_blocks = n // drows
    pair = lambda p, i: (layer, 0, jq(p))
    return pl.pallas_call(
        functools.partial(_ffn_kernel, nm=nm, nq=nq, bt=bt, rb=rb, last_half_valid=n_tiles % 2 == 0,
                          wd_valid_blocks=wd_valid_blocks, prompt_end=prompt_end,
                          sample_starts=sample_starts, seq_len=seq_len),
        grid=(nq + 1, nm),
        in_specs=[pl.BlockSpec((bm, d), lambda p, i: (row(p, i), 0)),
                  wspec_a, wspec_b, wspec_a, wspec_b,
                  pl.BlockSpec((None, drows, d),
                               lambda p, i: (layer, jnp.minimum(dblock(p, i), wd_valid_blocks - 1), 0)),
                  pl.BlockSpec((None, CONV_W, 2 * bt), pair),
                  pl.BlockSpec((None, 1, 2 * bt), pair),
                  pl.BlockSpec((n_seq, SUBLANES, 2 * bt), lambda p, i: (0, 0, jq(p)))],
        out_specs=[pl.BlockSpec((bm, 2 * bt), lambda p, i: (row(p, i), jq(p))),
                   pl.BlockSpec((n_seq + 1, SUBLANES, 2 * bt), lambda p, i: (0, 0, jq(p))),
                   pl.BlockSpec((drows, d), lambda p, i: (dblock(p, i), 0))],
        out_shape=[jax.ShapeDtypeStruct((m, n_pad), BF16),
                   jax.ShapeDtypeStruct((n_seq + 1, SUBLANES, n_pad), F32),
                   jax.ShapeDtypeStruct((n_pad, d), BF16)],
        scratch_shapes=[pltpu.VMEM((SUBLANES, 2 * bt), F32),
                        pltpu.VMEM((2, 2, d, bt), BF16),
                        pltpu.VMEM((2, 2, d, bt), BF16)],
        compiler_params=pltpu.CompilerParams(
            dimension_semantics=("arbitrary", "arbitrary"), vmem_limit_bytes=_vmem_limit(est)),
        name="ffn_gate_up",
    )(xb, wg, wg, wu, wu, wd, conv_w, conv_b, state8)


def kernel(x_prompt, x_sample, state_mlstm_C, state_mlstm_n, state_mlstm_m, state_hgrn_S, state_ffn_conv,
           ln_g, ln_b, w_a_in, b_a_gate, a_norm_w, w_a_out, w_b_in, b_lower, b_norm_w, w_b_out,
           w_ffn_gate, w_ffn_up, w_ffn_down, ffn_conv_w, ffn_conv_b):
    bp, tp, d = x_prompt.shape
    bs, ts, _ = x_sample.shape
    mp = bp * tp
    ms = bs * ts
    m = mp + ms
    assert bp == 1 and ts == CHUNK and d == D_MODEL
    l_prompt = 128

    x = jnp.concatenate([x_prompt.reshape(mp, d), x_sample.reshape(ms, d)], axis=0)
    xb = x.astype(BF16)
    w_a_in_t = jnp.swapaxes(w_a_in, 1, 2)
    ff_pad = -D_FF % (2 * FFN_BT)
    conv_w_p = jnp.pad(ffn_conv_w, ((0, 0), (0, 0), (0, ff_pad)))
    conv_b_p = jnp.pad(ffn_conv_b, ((0, 0), (0, ff_pad))).reshape(DEPTH, 1, D_FF + ff_pad)

    p_c, p_n, p_m, p_s, p_conv = [], [], [], [], []
    s_c, s_n, s_m, s_s, s_conv = [], [], [], [], []
    for l in range(DEPTH):
        j = l // 2
        if l % 2 == 0:
            w_gate = jnp.pad(w_a_in_t[j, A_MAIN:, :], ((0, LANES - 2 * NH_A), (0, 0)))
            b_gate = jnp.pad(b_a_gate[j], (0, LANES - 2 * NH_A)).reshape(1, LANES)
            proj = matmul_wcast(xb, w_a_in_t, j, n_cols=A_MAIN, w_t=True, bm=1088, bn=1024,
                                name="mlstm_in_proj")
            gates = mlstm_gates(xb, w_gate, b_gate)
            y_p, c_p, n_p, m_p = mlstm_scan(proj, gates, a_norm_w[j], row0=0, n_seq=1,
                                            n_chunk=mp // l_prompt, L=l_prompt)
            y_mix, c_s2, n_s2, m_s2 = mlstm_scan(
                proj, gates, a_norm_w[j], row0=mp, n_seq=bs, n_chunk=1, L=ts,
                init=(state_mlstm_C, state_mlstm_n, state_mlstm_m, j, y_p))
            p_c.append(c_p); p_n.append(n_p); p_m.append(m_p)
            s_c.append(c_s2); s_n.append(n_s2); s_m.append(m_s2)
            w_out = w_a_out
        else:
            proj = matmul_wcast(xb, w_b_in, j, bm=1088, bn=1024, name="hgrn_in_proj")
            y_mix, st = hgrn_scan(proj, b_lower, b_norm_w[j], state_hgrn_S, layer_j=j,
                                  n_prompt_chunks=mp // CHUNK)
            p_s.append(st[:1]); s_s.append(st[1:])
            w_out = w_b_out
        y = matmul_wcast(y_mix, w_out, j, bm=544, bn=1024, res=x, alpha=ALPHA, name="mixer_out_proj")
        x, xb = layer_norm(y, ln_g[l, 0], ln_b[l, 0])

        st8 = jnp.pad(state_ffn_conv[l], ((0, 0), (SUBLANES - (CONV_W - 1), 0), (0, ff_pad)))
        hmid, so, wd_b = ffn_gate_up(xb, w_ffn_gate, w_ffn_up, w_ffn_down, conv_w_p, conv_b_p, st8, l,
                                     bm=1088, bt=FFN_BT, rb=272, n_prompt_rows=mp, seq_len=ts)
        conv_new = so[:, SUBLANES - (CONV_W - 1):, :D_FF]
        p_conv.append(conv_new[:1]); s_conv.append(conv_new[1:])
        y = matmul_xres(hmid, wd_b, x, bm=544, bn=512, alpha=ALPHA, name="ffn_down_proj")
        if l < DEPTH - 1:
            x, xb = layer_norm(y, ln_g[l, 1], ln_b[l, 1])

    (y_prompt,) = layer_norm(y, ln_g[DEPTH - 1, 1], ln_b[DEPTH - 1, 1], rows=mp, emit_bf16=False)
    (y_sample,) = layer_norm(y, ln_g[DEPTH - 1, 1], ln_b[DEPTH - 1, 1], row0=mp, emit_bf16=False)
    y_prompt = y_prompt.reshape(bp, tp, d)
    y_sample = y_sample.reshape(bs, ts, d)
    return (y_prompt, y_sample,
            jnp.stack(p_c), jnp.stack(p_n), jnp.stack(p_m), jnp.stack(p_s), jnp.stack(p_conv),
            jnp.stack(s_c), jnp.stack(s_n), jnp.stack(s_m), jnp.stack(s_s), jnp.stack(s_conv))
```

```python
import functools

import jax
import jax.numpy as jnp
from jax import lax
from jax.experimental import pallas as pl
from jax.experimental.pallas import tpu as pltpu

F32 = jnp.float32
BF16 = jnp.bfloat16

D_MODEL = 4096
DEPTH = 4
CHUNK = 64
NH_A = 8
DK_A = 256
DV_A = 512
QK_A = NH_A * DK_A
VD_A = NH_A * DV_A
A_MAIN = 2 * QK_A + 2 * VD_A
GATE_SOFTCAP = 15.0
NH_B = 32
DK_B = 128
DV_B = 128
D_FF = 11008
FFN_BT = 256
CONV_W = 3
ALPHA = (2 * DEPTH) ** 0.25
LN_EPS = 1e-5
HEAD_EPS = 1e-6

LANES = 128
SUBLANES = 8
VMEM_LIMIT_CAP = 58 * 1024 * 1024


def _vmem_limit(nbytes):
    return int(min(VMEM_LIMIT_CAP, nbytes * 1.2 + (6 << 20)))


def _nbytes(shape, dtype):
    n = 1
    for s in shape:
        n *= s
    return n * jnp.dtype(dtype).itemsize


def _finish(acc, r_ref, o_ref, alpha):
    if r_ref is not None:
        acc = alpha * r_ref[...] + acc
    o_ref[...] = acc.astype(o_ref.dtype)


def _mm_wcast_kernel(*refs, nn, nm, w_t, has_res, alpha):
    if has_res:
        x_ref, wc_ref, r_ref, o_ref, wb_ref = refs
    else:
        x_ref, wc_ref, o_ref, wb_ref = refs
        r_ref = None
    p = pl.program_id(0)
    i = pl.program_id(1)
    crows = wc_ref.shape[0]

    @pl.when(p < nn)
    def _():
        wb_ref[p % 2, pl.ds(pl.multiple_of(i * crows, crows), crows), :] = wc_ref[...].astype(BF16)

    @pl.when(p > 0)
    def _():
        wb = wb_ref[(p + 1) % 2]
        if w_t:
            acc = lax.dot_general(x_ref[...], wb, (((1,), (1,)), ((), ())), preferred_element_type=F32)
        else:
            acc = jnp.dot(x_ref[...], wb, preferred_element_type=F32)
        _finish(acc, r_ref, o_ref, alpha)


def matmul_wcast(x, w, layer, *, bm, bn, n_cols=None, w_t=False, res=None, alpha=1.0, out_dtype=F32,
                 name="mm"):
    m, kdim = x.shape
    n = w.shape[1 if w_t else 2] if n_cols is None else n_cols
    nm, nn = m // bm, n // bn
    crows = (bn if w_t else kdim) // nm
    assert m % bm == 0 and n % bn == 0 and (bn if w_t else kdim) % nm == 0 and crows % SUBLANES == 0
    row = lambda p, i: jnp.where(p > 0, i, 0)
    xi = lambda p, i: (row(p, i), 0)
    oi = lambda p, i: (row(p, i), jnp.maximum(p - 1, 0))
    chunk = lambda p, i: jnp.where(p < nn, i, nm - 1)
    tile = lambda p: jnp.minimum(p, nn - 1)
    if w_t:
        wspec = pl.BlockSpec((None, crows, kdim), lambda p, i: (layer, tile(p) * nm + chunk(p, i), 0))
        wb_shape = (2, bn, kdim)
    else:
        wspec = pl.BlockSpec((None, crows, bn), lambda p, i: (layer, chunk(p, i), tile(p)))
        wb_shape = (2, kdim, bn)
    in_specs = [pl.BlockSpec((bm, kdim), xi), wspec]
    args = [x, w]
    est = 2 * (_nbytes((bm, kdim), BF16) + _nbytes(wspec.block_shape[1:], F32) + _nbytes((bm, bn), out_dtype))
    est += _nbytes((bm, bn), F32) + _nbytes(wb_shape, BF16)
    if res is not None:
        in_specs.append(pl.BlockSpec((bm, bn), oi))
        args.append(res)
        est += 2 * _nbytes((bm, bn), res.dtype)
    return pl.pallas_call(
        functools.partial(_mm_wcast_kernel, nn=nn, nm=nm, w_t=w_t, has_res=res is not None, alpha=alpha),
        grid=(nn + 1, nm),
        in_specs=in_specs,
        out_specs=pl.BlockSpec((bm, bn), oi),
        out_shape=jax.ShapeDtypeStruct((m, n), out_dtype),
        scratch_shapes=[pltpu.VMEM(wb_shape, BF16)],
        compiler_params=pltpu.CompilerParams(
            dimension_semantics=("arbitrary", "arbitrary"), vmem_limit_bytes=_vmem_limit(est)),
        name=name,
    )(*args)


def _mm_kernel(x_ref, w_ref, r_ref, o_ref, *, alpha):
    _finish(jnp.dot(x_ref[...], w_ref[...], preferred_element_type=F32), r_ref, o_ref, alpha)


def matmul_xres(x, w, res, *, bm, bn, alpha, name="mm"):
    m, kdim = x.shape
    n = w.shape[1]
    assert m % bm == 0 and n % bn == 0
    oi = lambda i, j: (i, j)
    est = 2 * (_nbytes((bm, kdim), BF16) + _nbytes((kdim, bn), BF16) + 2 * _nbytes((bm, bn), F32))
    est += _nbytes((bm, bn), F32)
    return pl.pallas_call(
        functools.partial(_mm_kernel, alpha=alpha),
        grid=(m // bm, n // bn),
        in_specs=[pl.BlockSpec((bm, kdim), lambda i, j: (i, 0)),
                  pl.BlockSpec((kdim, bn), lambda i, j: (0, j)),
                  pl.BlockSpec((bm, bn), oi)],
        out_specs=pl.BlockSpec((bm, bn), oi),
        out_shape=jax.ShapeDtypeStruct((m, n), F32),
        compiler_params=pltpu.CompilerParams(
            dimension_semantics=("arbitrary", "arbitrary"), vmem_limit_bytes=_vmem_limit(est)),
        name=name,
    )(x, w, res)


def _ln_kernel(y_ref, g_ref, b_ref, o_ref, ob_ref=None):
    y = y_ref[...]
    mu = jnp.mean(y, axis=-1, keepdims=True)
    yc = y - mu
    var = jnp.mean(yc * yc, axis=-1, keepdims=True)
    out = yc * lax.rsqrt(var + LN_EPS) * g_ref[...] + b_ref[...]
    o_ref[...] = out
    if ob_ref is not None:
        ob_ref[...] = out.astype(BF16)


def layer_norm(y, g, b, *, bm=512, row0=0, rows=None, emit_bf16=True):
    m, d = y.shape
    rows = m - row0 if rows is None else rows
    assert row0 % bm == 0 and rows % bm == 0
    blk0 = row0 // bm
    row = pl.BlockSpec((bm, d), lambda i: (i, 0))
    vec = pl.BlockSpec((1, d), lambda i: (0, 0))
    est = 2 * (2 * _nbytes((bm, d), F32) + _nbytes((bm, d), BF16)) + 2 * _nbytes((bm, d), F32)
    out_specs = [row, row] if emit_bf16 else [row]
    out_shape = [jax.ShapeDtypeStruct((rows, d), F32)]
    if emit_bf16:
        out_shape.append(jax.ShapeDtypeStruct((rows, d), BF16))
    return pl.pallas_call(
        _ln_kernel,
        grid=(rows // bm,),
        in_specs=[pl.BlockSpec((bm, d), lambda i: (blk0 + i, 0)), vec, vec],
        out_specs=out_specs,
        out_shape=out_shape,
        compiler_params=pltpu.CompilerParams(
            dimension_semantics=("arbitrary",), vmem_limit_bytes=_vmem_limit(est)),
        name="layer_norm",
    )(y, g.reshape(1, d), b.reshape(1, d))


def _log_sigmoid(z):
    return jnp.minimum(z, 0.0) - jnp.log1p(jnp.exp(-jnp.abs(z)))


def _gates_kernel(x_ref, w_ref, b_ref, o_ref):
    g = lax.dot_general(x_ref[...], w_ref[...].astype(BF16), (((1,), (1,)), ((), ())),
                        preferred_element_type=F32)
    g = g + b_ref[...]
    sc = GATE_SOFTCAP * jnp.tanh(g / GATE_SOFTCAP)
    lane = lax.broadcasted_iota(jnp.int32, sc.shape, 1)
    o_ref[...] = jnp.where(lane < NH_A, sc, _log_sigmoid(sc))


def mlstm_gates(xb, w_gate_t, b_gate, *, bm=1088):
    m, d = xb.shape
    est = 2 * (_nbytes((bm, d), BF16) + 2 * _nbytes((LANES, d), F32) + _nbytes((bm, LANES), F32))
    return pl.pallas_call(
        _gates_kernel,
        grid=(m // bm,),
        in_specs=[pl.BlockSpec((bm, d), lambda i: (i, 0)),
                  pl.BlockSpec((LANES, d), lambda i: (0, 0)),
                  pl.BlockSpec((1, LANES), lambda i: (0, 0))],
        out_specs=pl.BlockSpec((bm, LANES), lambda i: (i, 0)),
        out_shape=jax.ShapeDtypeStruct((m, LANES), F32),
        compiler_params=pltpu.CompilerParams(
            dimension_semantics=("arbitrary",), vmem_limit_bytes=_vmem_limit(est)),
        name="mlstm_gates",
    )(xb, w_gate_t, b_gate)


def _mlstm_kernel(*refs, L, has_init):
    if has_init:
        (q_ref, k_ref, v_ref, o_ref, gc_ref, gr_ref, nw_ref, c0_ref, n0_ref, m0_ref, y_prev_ref,
         y_ref, ct_ref, nt_ref, mt_ref, c_s, n_s, m_s) = refs
    else:
        (q_ref, k_ref, v_ref, o_ref, gc_ref, gr_ref, nw_ref,
         y_ref, ct_ref, nt_ref, mt_ref, c_s, n_s, m_s) = refs
    c = pl.program_id(1)
    nc = pl.num_programs(1)

    @pl.when(c == 0)
    def _():
        if has_init:
            c_s[...] = c0_ref[0]
            n_s[...] = n0_ref[0]
            m_s[...] = m0_ref[0]
        else:
            c_s[...] = jnp.zeros_like(c_s)
            n_s[...] = jnp.zeros_like(n_s)
            m_s[...] = jnp.zeros_like(m_s)

    t_idx = lax.broadcasted_iota(jnp.int32, (L, L), 0)
    s_idx = lax.broadcasted_iota(jnp.int32, (L, L), 1)
    causal = s_idx <= t_idx
    lane = lax.broadcasted_iota(jnp.int32, (L, LANES), 1)
    gcol = gc_ref[...]

    def head(h, carry):
        qs = pl.ds(pl.multiple_of(h * DK_A, DK_A), DK_A)
        vs = pl.ds(pl.multiple_of(h * DV_A, DV_A), DV_A)
        q = q_ref[:, qs]
        k = k_ref[:, qs] * (DK_A ** -0.5)
        v = v_ref[:, vs]
        qb = q.astype(BF16)
        kb = k.astype(BF16)
        vb = v.astype(BF16)
        ig_r = gr_ref[h, 0]
        lf_r = gr_ref[NH_A + h, 0]
        ig_c = jnp.sum(jnp.where(lane == h, gcol, 0.0), axis=1, keepdims=True)
        lf_c = jnp.sum(jnp.where(lane == NH_A + h, gcol, 0.0), axis=1, keepdims=True)
        f_c = jnp.sum(jnp.where(causal, lf_r, 0.0), axis=1, keepdims=True)
        f_r = jnp.sum(jnp.where(t_idx <= s_idx, lf_c, 0.0), axis=0, keepdims=True)
        f_end = jnp.sum(lf_r, axis=1, keepdims=True)
        m_prev = m_s[h]
        c_prev = c_s[h]
        n_prev = n_s[h]

        d = jnp.where(causal, f_c - f_r + ig_r, -jnp.inf)
        g = f_c + m_prev
        m_t = jnp.maximum(g, jnp.max(d, axis=1, keepdims=True))
        w = jnp.exp(d - m_t)
        inter = jnp.exp(g - m_t)
        s = lax.dot_general(qb, kb, (((1,), (1,)), ((), ())), preferred_element_type=F32) * w
        num = jnp.dot(s.astype(BF16), vb, preferred_element_type=F32)
        num = num + inter * jnp.dot(qb, c_prev.astype(BF16), preferred_element_type=F32)
        den = jnp.sum(s, axis=1, keepdims=True) + inter * jnp.sum(q * n_prev, axis=1, keepdims=True)
        hh = num / jnp.maximum(jnp.abs(den), jnp.exp(-m_t))

        d_end_r = f_end - f_r + ig_r
        d_end_c = f_end - f_c + ig_c
        m_new = jnp.maximum(f_end + m_prev, jnp.max(d_end_r, axis=1, keepdims=True))
        w_end_c = jnp.exp(d_end_c - m_new)
        decay = jnp.exp(f_end + m_prev - m_new)
        kw = k * w_end_c
        c_s[h] = decay * c_prev + lax.dot_general(
            kw.astype(BF16), vb, (((0,), (0,)), ((), ())), preferred_element_type=F32)
        n_s[h] = decay * n_prev + jnp.sum(kw, axis=0, keepdims=True)
        m_s[h] = m_new

        mu = jnp.mean(hh, axis=1, keepdims=True)
        hc = hh - mu
        hn = hc * lax.rsqrt(jnp.mean(hc * hc, axis=1, keepdims=True) + HEAD_EPS)
        hn = hn * nw_ref[:, vs]
        y_ref[:, vs] = (jax.nn.sigmoid(o_ref[:, vs]) * hn).astype(BF16)
        return carry

    lax.fori_loop(0, NH_A, head, 0, unroll=2)

    @pl.when(c == nc - 1)
    def _():
        ct_ref[0] = c_s[...]
        nt_ref[0] = n_s[...]
        mt_ref[0] = m_s[...]


def mlstm_scan(proj, gates, norm_w, *, row0, n_seq, n_chunk, L, init=None):
    m = proj.shape[0]
    rb0 = row0 // L
    rows = n_seq * n_chunk * L
    g_rows = lax.slice_in_dim(gates, row0, row0 + rows, axis=0)
    g_t = g_rows[:, :2 * NH_A].T.reshape(2 * NH_A, n_seq * n_chunk, 1, L)
    rmap = lambda s, c: (rb0 + s * n_chunk + c)
    in_specs = [
        pl.BlockSpec((L, QK_A), lambda s, c: (rmap(s, c), 0)),
        pl.BlockSpec((L, QK_A), lambda s, c: (rmap(s, c), 1)),
        pl.BlockSpec((L, VD_A), lambda s, c: (rmap(s, c), 1)),
        pl.BlockSpec((L, VD_A), lambda s, c: (rmap(s, c), 2)),
        pl.BlockSpec((L, LANES), lambda s, c: (rmap(s, c), 0)),
        pl.BlockSpec((2 * NH_A, 1, 1, L), lambda s, c: (0, s * n_chunk + c, 0, 0)),
        pl.BlockSpec((1, VD_A), lambda s, c: (0, 0)),
    ]
    args = [proj, proj, proj, proj, gates, g_t, norm_w.reshape(1, VD_A)]
    state_specs = [
        pl.BlockSpec((1, NH_A, DK_A, DV_A), lambda s, c: (s, 0, 0, 0)),
        pl.BlockSpec((1, NH_A, 1, DK_A), lambda s, c: (s, 0, 0, 0)),
        pl.BlockSpec((1, NH_A, 1, 1), lambda s, c: (s, 0, 0, 0)),
    ]
    io_alias = {}
    if init is not None:
        c0, n0, m0, layer, y_prev = init
        nl = c0.shape[0]
        in_specs += [
            pl.BlockSpec((None, 1, NH_A, DK_A, DV_A), lambda s, c: (layer, s, 0, 0, 0)),
            pl.BlockSpec((None, 1, NH_A, 1, DK_A), lambda s, c: (layer, s, 0, 0, 0)),
            pl.BlockSpec((None, 1, NH_A, 1, 1), lambda s, c: (layer, s, 0, 0, 0)),
            pl.BlockSpec(memory_space=pl.ANY),
        ]
        args += [c0, n0.reshape(nl, n_seq, NH_A, 1, DK_A), m0.reshape(nl, n_seq, NH_A, 1, 1), y_prev]
        io_alias = {len(args) - 1: 0}
    est = 2 * (2 * _nbytes((L, QK_A), F32) + 2 * _nbytes((L, VD_A), F32) + _nbytes((L, VD_A), BF16))
    est += (3 if init is None else 5) * _nbytes((NH_A, DK_A, DV_A), F32)
    est += 8 * _nbytes((L, max(L, DV_A)), F32)
    y, ct, nt, mt = pl.pallas_call(
        functools.partial(_mlstm_kernel, L=L, has_init=init is not None),
        grid=(n_seq, n_chunk),
        in_specs=in_specs,
        out_specs=[pl.BlockSpec((L, VD_A), lambda s, c: (rmap(s, c), 0))] + state_specs,
        out_shape=[
            jax.ShapeDtypeStruct((m, VD_A), BF16),
            jax.ShapeDtypeStruct((n_seq, NH_A, DK_A, DV_A), F32),
            jax.ShapeDtypeStruct((n_seq, NH_A, 1, DK_A), F32),
            jax.ShapeDtypeStruct((n_seq, NH_A, 1, 1), F32),
        ],
        scratch_shapes=[
            pltpu.VMEM((NH_A, DK_A, DV_A), F32),
            pltpu.VMEM((NH_A, 1, DK_A), F32),
            pltpu.VMEM((NH_A, 1, 1), F32),
        ],
        input_output_aliases=io_alias,
        compiler_params=pltpu.CompilerParams(
            dimension_semantics=("arbitrary", "arbitrary"), vmem_limit_bytes=_vmem_limit(est)),
        name="mlstm_scan_init" if init is not None else "mlstm_scan",
    )(*args)
    return y, ct, nt.reshape(n_seq, NH_A, DK_A), mt.reshape(n_seq, NH_A)


def _hgrn_kernel(q_ref, f_ref, i_ref, g_ref, bl_ref, nw_ref, s0_ref, y_ref, st_ref, s_s,
                 *, layer_j, n_prompt_chunks, pair_unroll):
    L = CHUNK
    W = 2 * DK_B
    n_pairs = NH_B // 2
    g_id = pl.program_id(0)

    @pl.when(g_id == 0)
    def _():
        s_s[...] = jnp.zeros_like(s_s)

    @pl.when(g_id >= n_prompt_chunks)
    def _():
        def load(hp, carry):
            s_s[hp] = jnp.concatenate([s0_ref[0, 2 * hp].T, s0_ref[0, 2 * hp + 1].T], axis=1)
            return carry
        lax.fori_loop(0, n_pairs, load, 0)

    row8 = lax.broadcasted_iota(jnp.int32, (SUBLANES, W), 0)
    t_idx = lax.broadcasted_iota(jnp.int32, (2 * L, 2 * L), 0)
    s_idx = lax.broadcasted_iota(jnp.int32, (2 * L, 2 * L), 1)
    levels = [1, 2, 4, 8, 16, 32]
    n_grp = L // SUBLANES
    groups = lambda arr: [arr[b * SUBLANES:(b + 1) * SUBLANES] for b in range(n_grp)]
    upper8 = {mm: (row8 & mm) != 0 for mm in levels if mm < SUBLANES}
    pair = {mm: ((t_idx ^ s_idx) < 2 * mm) & ((t_idx & mm) != 0) & ((s_idx & mm) == 0) for mm in levels}
    diag = t_idx == s_idx
    nt_dims = (((1,), (1,)), ((), ()))
    tn_dims = (((0,), (0,)), ((), ()))
    zero_half = jnp.zeros((L, DK_B), BF16)

    def block_diag(xw):
        top = jnp.concatenate([xw[:, :DK_B], zero_half], axis=1)
        bot = jnp.concatenate([zero_half, xw[:, DK_B:]], axis=1)
        return jnp.concatenate([top, bot], axis=0)

    def stack_heads(xw):
        return jnp.concatenate([xw[:, :DK_B], xw[:, DK_B:]], axis=0)

    def head(hp, carry):
        sl = pl.ds(pl.multiple_of(hp * W, W), W)
        q = q_ref[:, sl]
        fz = f_ref[:, sl]
        inp = i_ref[:, sl]
        gate = g_ref[:, sl]
        bl = bl_ref[:, sl]
        e = jnp.exp(bl - jnp.max(bl, axis=0, keepdims=True))
        p = e / jnp.sum(e, axis=0, keepdims=True)
        lb = jnp.zeros((1, W), F32)
        for r in range(1, layer_j + 1):
            lb = lb + p[r:r + 1]
        sig = jax.nn.sigmoid(fz)
        oml = 1.0 - lb
        logf = jnp.log(lb + oml * sig)
        k = oml * (1.0 - sig)

        ib = stack_heads(inp.astype(BF16))
        s_acc = jnp.where(
            diag, lax.dot_general(block_diag(q.astype(BF16)), block_diag(k.astype(BF16)), nt_dims,
                                  preferred_element_type=F32), 0.0)
        q8, k8 = groups(q), groups(k)
        p_sum = groups(logf)
        r_sum = [jnp.zeros((SUBLANES, W), F32)] * n_grp
        g_sum = list(p_sum)
        for mm in levels:
            xs, g_new = [], []
            for b in range(n_grp):
                if mm < SUBLANES:
                    up = upper8[mm]
                    sib = jnp.where(up, pltpu.roll(g_sum[b], mm, 0), pltpu.roll(g_sum[b], SUBLANES - mm, 0))
                    xs.append(jnp.where(up, q8[b], k8[b]) * jnp.exp(jnp.where(up, p_sum[b], r_sum[b])))
                    p_sum[b] = p_sum[b] + jnp.where(up, sib, 0.0)
                    r_sum[b] = r_sum[b] + jnp.where(up, 0.0, sib)
                else:
                    bit = mm // SUBLANES
                    sib = g_sum[b ^ bit]
                    if b & bit:
                        xs.append(q8[b] * jnp.exp(p_sum[b]))
                        p_sum[b] = p_sum[b] + sib
                    else:
                        xs.append(k8[b] * jnp.exp(r_sum[b]))
                        r_sum[b] = r_sum[b] + sib
                g_new.append(g_sum[b] + sib)
            g_sum = g_new
            x = block_diag(jnp.concatenate(xs, axis=0).astype(BF16))
            s_m = lax.dot_general(x, x, nt_dims, preferred_element_type=F32)
            s_acc = jnp.where(pair[mm], s_m, s_acc)
        p_sum = jnp.concatenate(p_sum, axis=0)
        r_sum = jnp.concatenate(r_sum, axis=0)
        g_sum = g_sum[0]
        s_prev = s_s[hp]
        o = lax.dot_general(block_diag((q * jnp.exp(p_sum)).astype(BF16)), s_prev.astype(BF16), nt_dims,
                            preferred_element_type=F32)
        o = o + jnp.dot(s_acc.astype(BF16), ib, preferred_element_type=F32)
        k_end = block_diag((k * jnp.exp(r_sum)).astype(BF16))
        s_s[hp] = jnp.exp(g_sum[0:1]) * s_prev + lax.dot_general(ib, k_end, tn_dims, preferred_element_type=F32)

        o = o * lax.rsqrt(jnp.mean(o * o, axis=1, keepdims=True) + HEAD_EPS)
        post = nw_ref[:, sl] * (gate * jax.nn.sigmoid(gate))
        y_ref[:, sl] = jnp.concatenate([o[:L] * post[:, :DV_B], o[L:] * post[:, DV_B:]], axis=1).astype(BF16)
        return carry

    lax.fori_loop(0, n_pairs, head, 0, unroll=pair_unroll)

    @pl.when(g_id >= n_prompt_chunks - 1)
    def _():
        def store(hp, carry):
            st = s_s[hp]
            st_ref[0, 2 * hp] = st[:, :DK_B].T
            st_ref[0, 2 * hp + 1] = st[:, DK_B:].T
            return carry
        lax.fori_loop(0, n_pairs, store, 0)


def hgrn_scan(proj, b_lower, norm_w, s0, *, layer_j, n_prompt_chunks, pair_unroll=16):
    m = proj.shape[0]
    n_chunks = m // CHUNK
    n_seq = n_chunks - n_prompt_chunks + 1
    n_b = b_lower.shape[0]
    col = lambda cb: pl.BlockSpec((CHUNK, D_MODEL), lambda g: (g, cb))
    est = 2 * (4 * _nbytes((CHUNK, D_MODEL), F32) + _nbytes((CHUNK, D_MODEL), BF16))
    est += 5 * _nbytes((NH_B, DK_B, DV_B), F32)
    y, st = pl.pallas_call(
        functools.partial(_hgrn_kernel, layer_j=layer_j, n_prompt_chunks=n_prompt_chunks,
                          pair_unroll=pair_unroll),
        grid=(n_chunks,),
        in_specs=[col(0), col(1), col(2), col(3),
                  pl.BlockSpec((n_b, D_MODEL), lambda g: (0, 0)),
                  pl.BlockSpec((1, D_MODEL), lambda g: (0, 0)),
                  pl.BlockSpec((None, 1, NH_B, DK_B, DV_B),
                               lambda g: (layer_j, jnp.maximum(g - n_prompt_chunks, 0), 0, 0, 0))],
        out_specs=[pl.BlockSpec((CHUNK, D_MODEL), lambda g: (g, 0)),
                   pl.BlockSpec((1, NH_B, DK_B, DV_B),
                                lambda g: (jnp.maximum(g - (n_prompt_chunks - 1), 0), 0, 0, 0))],
        out_shape=[jax.ShapeDtypeStruct((m, D_MODEL), BF16),
                   jax.ShapeDtypeStruct((n_seq, NH_B, DK_B, DV_B), F32)],
        scratch_shapes=[pltpu.VMEM((NH_B // 2, DV_B, 2 * DK_B), F32)],
        compiler_params=pltpu.CompilerParams(
            dimension_semantics=("arbitrary",), vmem_limit_bytes=_vmem_limit(est)),
        name="hgrn_scan",
    )(proj, proj, proj, proj, b_lower, norm_w.reshape(1, D_MODEL), s0)
    return y, st


def _ffn_kernel(x_ref, wga_ref, wgb_ref, wua_ref, wub_ref, wdc_ref, cw_ref, cb_ref, st_ref,
                h_ref, so_ref, wdo_ref, carry_ref, wgs_ref, wus_ref,
                *, nm, nq, bt, rb, last_half_valid, wd_valid_blocks, prompt_end, sample_starts, seq_len):
    p = pl.program_id(0)
    i = pl.program_id(1)
    bm = x_ref.shape[0]
    crows = wga_ref.shape[0]

    def cast_next_pair():
        rows = pl.ds(pl.multiple_of(i * crows, crows), crows)
        for half, (g_ref, u_ref) in enumerate(((wga_ref, wua_ref), (wgb_ref, wub_ref))):
            wgs_ref[p % 2, half, rows, :] = g_ref[...].astype(BF16)
            wus_ref[p % 2, half, rows, :] = u_ref[...].astype(BF16)

    @pl.when(p == 0)
    def _():
        cast_next_pair()

    cols = [slice(half * bt, (half + 1) * bt) for half in range(2)]

    def compute(n_half, cast_ahead):
        slot = (p + 1) % 2
        rows8 = lax.broadcasted_iota(jnp.int32, (SUBLANES, bt), 0)
        is_last = i == nm - 1
        w0 = [cw_ref[0:1, c] for c in cols]
        w1 = [cw_ref[1:2, c] for c in cols]
        w2 = [cw_ref[2:3, c] for c in cols]
        cb = [cb_ref[:, c] for c in cols]

        def act(half, a_cur, a_p1, a_p2, u_cur):
            c = cb[half] + a_p2 * w0[half]
            c = c + a_p1 * w1[half]
            c = c + a_cur * w2[half]
            return (c * jax.nn.sigmoid(c) * u_cur).astype(BF16)

        prev8 = [jnp.where(i == 0, 0.0, carry_ref[:, c]) for c in cols]
        for r0 in range(0, bm, rb):
            xs = x_ref[r0:r0 + rb, :]
            for half in range(n_half):
                c = cols[half]
                a = jnp.dot(xs, wgs_ref[slot, half], preferred_element_type=F32)
                u = jnp.dot(xs, wus_ref[slot, half], preferred_element_type=F32)
                h_ref[r0:r0 + rb, c] = act(half, a, pltpu.roll(a, 1, 0), pltpu.roll(a, 2, 0), u)

                def patch(off, before8):
                    blk = a[off:off + SUBLANES]
                    p1 = jnp.where(rows8 < 1, pltpu.roll(before8, 1, 0), pltpu.roll(blk, 1, 0))
                    p2 = jnp.where(rows8 < 2, pltpu.roll(before8, 2, 0), pltpu.roll(blk, 2, 0))
                    h_ref[r0 + off:r0 + off + SUBLANES, c] = act(half, blk, p1, p2, u[off:off + SUBLANES])

                patch(0, prev8[half])
                for s, start in enumerate(sample_starts):
                    off = start - r0
                    if SUBLANES <= off < rb:
                        patch(off, jnp.where(is_last, st_ref[s, :, c], a[off - SUBLANES:off]))
                    end = off + seq_len
                    if SUBLANES <= end <= rb:
                        so_ref[1 + s, :, c] = a[end - SUBLANES:end]
                if SUBLANES <= prompt_end - r0 <= rb:
                    so_ref[0, :, c] = a[prompt_end - r0 - SUBLANES:prompt_end - r0]
                prev8[half] = a[rb - SUBLANES:rb]
        for half in range(n_half):
            carry_ref[:, cols[half]] = prev8[half]

        if cast_ahead:
            cast_next_pair()
        wd_rows_valid = (p - 1) * nm + i < wd_valid_blocks
        wdo_ref[...] = jnp.where(wd_rows_valid, wdc_ref[...], 0.0).astype(BF16)

    @pl.when((p > 0) & (p < nq))
    def _():
        compute(2, True)

    @pl.when(p == nq)
    def _():
        if last_half_valid:
            compute(2, False)
        else:
            compute(1, False)
            h_ref[:, cols[1]] = jnp.zeros((bm, bt), BF16)
            so_ref[:, :, cols[1]] = jnp.zeros((so_ref.shape[0], SUBLANES, bt), F32)


def ffn_gate_up(xb, wg, wu, wd, conv_w, conv_b, state8, layer, *, bm, bt, rb, n_prompt_rows, seq_len):
    m, d = xb.shape
    n = wg.shape[2]
    nm, n_tiles = m // bm, n // bt
    nq = -(-n_tiles // 2)
    n_pad = nq * 2 * bt
    crows = d // nm
    drows = 2 * bt // nm
    n_seq = state8.shape[0]
    tile0 = (nm - 1) * bm
    prompt_end = n_prompt_rows - tile0
    sample_starts = tuple(prompt_end + s * seq_len for s in range(n_seq))
    assert m % bm == 0 and n % bt == 0 and bm % rb == 0 and rb % SUBLANES == 0
    assert d % nm == 0 and crows % SUBLANES == 0 and (2 * bt) % nm == 0 and drows % SUBLANES == 0
    assert n % drows == 0 and conv_w.shape[2] == n_pad and conv_b.shape[2] == n_pad and state8.shape[2] == n_pad
    assert m == n_prompt_rows + n_seq * seq_len and SUBLANES <= prompt_end
    assert all(st % SUBLANES == 0 and st % rb != 0 for st in sample_starts)
    est = 2 * (_nbytes((bm, d), BF16) + 4 * _nbytes((crows, bt), F32) + _nbytes((bm, 2 * bt), BF16))
    est += 2 * (_nbytes((drows, d), F32) + _nbytes((drows, d), BF16))
    est += 8 * _nbytes((d, bt), BF16)
    est += 6 * _nbytes((rb, bt), F32) + 4 * _nbytes((n_seq + 1, SUBLANES, 2 * bt), F32)
    row = lambda p, i: jnp.where(p > 0, i, 0)
    jq = lambda p: jnp.maximum(p - 1, 0)
    chunk = lambda p, i: jnp.where(p < nq, i, nm - 1)
    tile_a = lambda p: 2 * jnp.minimum(p, nq - 1)
    tile_b = lambda p: jnp.minimum(tile_a(p) + 1, n_tiles - 1)
    wspec_a = pl.BlockSpec((None, crows, bt), lambda p, i: (layer, chunk(p, i), tile_a(p)))
    wspec_b = pl.BlockSpec((None, crows, bt), lambda p, i: (layer, chunk(p, i), tile_b(p)))
    dblock = lambda p, i: jq(p) * nm + row(p, i)
    wd_valid_blocks = n // drows
    pair = lambda p, i: (layer, 0, jq(p))
    return pl.pallas_call(
        functools.partial(_ffn_kernel, nm=nm, nq=nq, bt=bt, rb=rb, last_half_valid=n_tiles % 2 == 0,
                          wd_valid_blocks=wd_valid_blocks, prompt_end=prompt_end,
                          sample_starts=sample_starts, seq_len=seq_len),
        grid=(nq + 1, nm),
        in_specs=[pl.BlockSpec((bm, d), lambda p, i: (row(p, i), 0)),
                  wspec_a, wspec_b, wspec_a, wspec_b,
                  pl.BlockSpec((None, drows, d),
                               lambda p, i: (layer, jnp.minimum(dblock(p, i), wd_valid_blocks - 1), 0)),
                  pl.BlockSpec((None, CONV_W, 2 * bt), pair),
                  pl.BlockSpec((None, 1, 2 * bt), pair),
                  pl.BlockSpec((n_seq, SUBLANES, 2 * bt), lambda p, i: (0, 0, jq(p)))],
        out_specs=[pl.BlockSpec((bm, 2 * bt), lambda p, i: (row(p, i), jq(p))),
                   pl.BlockSpec((n_seq + 1, SUBLANES, 2 * bt), lambda p, i: (0, 0, jq(p))),
                   pl.BlockSpec((drows, d), lambda p, i: (dblock(p, i), 0))],
        out_shape=[jax.ShapeDtypeStruct((m, n_pad), BF16),
                   jax.ShapeDtypeStruct((n_seq + 1, SUBLANES, n_pad), F32),
                   jax.ShapeDtypeStruct((n_pad, d), BF16)],
        scratch_shapes=[pltpu.VMEM((SUBLANES, 2 * bt), F32),
                        pltpu.VMEM((2, 2, d, bt), BF16),
                        pltpu.VMEM((2, 2, d, bt), BF16)],
        compiler_params=pltpu.CompilerParams(
            dimension_semantics=("arbitrary", "arbitrary"), vmem_limit_bytes=_vmem_limit(est)),
        name="ffn_gate_up",
    )(xb, wg, wg, wu, wu, wd, conv_w, conv_b, state8)


def kernel(x_prompt, x_sample, state_mlstm_C, state_mlstm_n, state_mlstm_m, state_hgrn_S, state_ffn_conv,
           ln_g, ln_b, w_a_in, b_a_gate, a_norm_w, w_a_out, w_b_in, b_lower, b_norm_w, w_b_out,
           w_ffn_gate, w_ffn_up, w_ffn_down, ffn_conv_w, ffn_conv_b):
    bp, tp, d = x_prompt.shape
    bs, ts, _ = x_sample.shape
    mp = bp * tp
    ms = bs * ts
    m = mp + ms
    assert bp == 1 and ts == CHUNK and d == D_MODEL
    l_prompt = 128

    x = jnp.concatenate([x_prompt.reshape(mp, d), x_sample.reshape(ms, d)], axis=0)
    xb = x.astype(BF16)
    w_a_in_t = jnp.swapaxes(w_a_in, 1, 2)
    ff_pad = -D_FF % (2 * FFN_BT)
    conv_w_p = jnp.pad(ffn_conv_w, ((0, 0), (0, 0), (0, ff_pad)))
    conv_b_p = jnp.pad(ffn_conv_b, ((0, 0), (0, ff_pad))).reshape(DEPTH, 1, D_FF + ff_pad)

    p_c, p_n, p_m, p_s, p_conv = [], [], [], [], []
    s_c, s_n, s_m, s_s, s_conv = [], [], [], [], []
    for l in range(DEPTH):
        j = l // 2
        if l % 2 == 0:
            w_gate = jnp.pad(w_a_in_t[j, A_MAIN:, :], ((0, LANES - 2 * NH_A), (0, 0)))
            b_gate = jnp.pad(b_a_gate[j], (0, LANES - 2 * NH_A)).reshape(1, LANES)
            proj = matmul_wcast(xb, w_a_in_t, j, n_cols=A_MAIN, w_t=True, bm=1088, bn=1024,
                                name="mlstm_in_proj")
            gates = mlstm_gates(xb, w_gate, b_gate)
            y_p, c_p, n_p, m_p = mlstm_scan(proj, gates, a_norm_w[j], row0=0, n_seq=1,
                                            n_chunk=mp // l_prompt, L=l_prompt)
            y_mix, c_s2, n_s2, m_s2 = mlstm_scan(
                proj, gates, a_norm_w[j], row0=mp, n_seq=bs, n_chunk=1, L=ts,
                init=(state_mlstm_C, state_mlstm_n, state_mlstm_m, j, y_p))
            p_c.append(c_p); p_n.append(n_p); p_m.append(m_p)
            s_c.append(c_s2); s_n.append(n_s2); s_m.append(m_s2)
            w_out = w_a_out
        else:
            proj = matmul_wcast(xb, w_b_in, j, bm=1088, bn=1024, name="hgrn_in_proj")
            y_mix, st = hgrn_scan(proj, b_lower, b_norm_w[j], state_hgrn_S, layer_j=j,
                                  n_prompt_chunks=mp // CHUNK)
            p_s.append(st[:1]); s_s.append(st[1:])
            w_out = w_b_out
        y = matmul_wcast(y_mix, w_out, j, bm=544, bn=1024, res=x, alpha=ALPHA, name="mixer_out_proj")
        x, xb = layer_norm(y, ln_g[l, 0], ln_b[l, 0])

        st8 = jnp.pad(state_ffn_conv[l], ((0, 0), (SUBLANES - (CONV_W - 1), 0), (0, ff_pad)))
        hmid, so, wd_b = ffn_gate_up(xb, w_ffn_gate, w_ffn_up, w_ffn_down, conv_w_p, conv_b_p, st8, l,
                                     bm=1088, bt=FFN_BT, rb=272, n_prompt_rows=mp, seq_len=ts)
        conv_new = so[:, SUBLANES - (CONV_W - 1):, :D_FF]
        p_conv.append(conv_new[:1]); s_conv.append(conv_new[1:])
        y = matmul_xres(hmid, wd_b, x, bm=544, bn=512, alpha=ALPHA, name="ffn_down_proj")
        if l < DEPTH - 1:
            x, xb = layer_norm(y, ln_g[l, 1], ln_b[l, 1])

    (y_prompt,) = layer_norm(y, ln_g[DEPTH - 1, 1], ln_b[DEPTH - 1, 1], rows=mp, emit_bf16=False)
    (y_sample,) = layer_norm(y, ln_g[DEPTH - 1, 1], ln_b[DEPTH - 1, 1], row0=mp, emit_bf16=False)
    y_prompt = y_prompt.reshape(bp, tp, d)
    y_sample = y_sample.reshape(bs, ts, d)
    return (y_prompt, y_sample,
            jnp.stack(p_c), jnp.stack(p_n), jnp.stack(p_m), jnp.stack(p_s), jnp.stack(p_conv),
            jnp.stack(s_c), jnp.stack(s_n), jnp.stack(s_m), jnp.stack(s_s), jnp.stack(s_conv))
```
